```python
import jax, jax.numpy as jnp
from jax import lax
import numpy as np

D_MODEL = 1024
BATCH = 8
SEQ = 2048
DEPTH = 2
DEC_BATCH = 128
DEC_SEQ = 4
PAST_LEN = 2048
PAGE_SIZE = 128

N_EVEN = (DEPTH + 1) // 2
N_ODD = DEPTH // 2

SSM_HEADS = 16
SSM_HEAD_DIM = 64
D_SSM = SSM_HEADS * SSM_HEAD_DIM
SSM_GROUPS = 2
D_STATE = 128
SSD_CONV = 4
SSD_CHUNK = 128
CONV_DIM = D_SSM + 2 * SSM_GROUPS * D_STATE
ATT_HEADS = 16
ATT_HEAD_DIM = 64
D_ATT = ATT_HEADS * ATT_HEAD_DIM
Q_BLOCK = 128
SB_SCALE = ATT_HEAD_DIM ** -0.5
SB_BIAS_INIT = -6.0
_S0 = D_SSM
_S1 = _S0 + CONV_DIM
_S2 = _S1 + SSM_HEADS
_S3 = _S2 + D_ATT
_S4 = _S3 + D_ATT
SPLIT_EVEN = (_S0, _S1, _S2, _S3, _S4)
IN_EVEN = _S4 + D_ATT
D_SC = D_MODEL
SC_WIDTH = 3
N_EXPERT_GROUPS = 4
EXPERTS_PER_GROUP = 8
N_EXPERTS = N_EXPERT_GROUPS * EXPERTS_PER_GROUP
TOP_K = 2
D_EXPERT = 256
RMS_EPS = 1e-6

kernel_name = "hybrid_ssd_stickbreak_shortconv_hmoe_step"


def rmsnorm(x, g):
    xf = x.astype(jnp.float32)
    y = xf * lax.rsqrt(jnp.mean(xf * xf, axis=-1, keepdims=True) + RMS_EPS)
    return (y * g.astype(jnp.float32)).astype(x.dtype)


def group_rmsnorm(y, g):
    shp = y.shape
    yg = y.reshape(shp[:-1] + (SSM_GROUPS, D_SSM // SSM_GROUPS))
    yg = yg * lax.rsqrt(jnp.mean(yg * yg, axis=-1, keepdims=True) + RMS_EPS)
    return yg.reshape(shp) * g.astype(jnp.float32)


def ada_modulation(c, w, b):
    mod = jax.nn.silu(c) @ w + b
    return jnp.split(mod[:, None, :], 6, axis=-1)


def modulate(h, shift, scale):
    return h * (1.0 + scale) + shift


def causal_dwconv(u, buf, w):
    width, length = w.shape[0], u.shape[1]
    up = jnp.concatenate([buf.astype(u.dtype), u], axis=1)
    out = up[:, :length] * w[0]
    for j in range(1, width):
        out = out + up[:, j:j + length] * w[j]
    return out, up[:, -(width - 1):]


def ssd_scan(x, dt, a, bmat, cmat, h0, chunk):
    f32 = jnp.float32
    bsz, length, nh, hp = x.shape
    ng, ns = bmat.shape[-2:]
    rep = nh // ng
    nc = length // chunk
    xr = x.astype(f32).reshape(bsz, nc, chunk, ng, rep, hp)
    dtr = dt.reshape(bsz, nc, chunk, ng, rep)
    br = bmat.astype(f32).reshape(bsz, nc, chunk, ng, ns)
    cr = cmat.astype(f32).reshape(bsz, nc, chunk, ng, ns)
    a_cum = jnp.cumsum(dtr * a.reshape(ng, rep), axis=2)
    xdt = xr * dtr[..., None]
    seg = a_cum[:, :, :, None] - a_cum[:, :, None, :]
    tri = jnp.tril(jnp.ones((chunk, chunk), bool))[:, :, None, None]
    decay = jnp.exp(jnp.where(tri, seg, -jnp.inf))
    cb = jnp.einsum('bclgn,bcsgn->bclsg', cr, br)
    y_diag = jnp.einsum('bclsgr,bcsgrp->bclgrp', cb[..., None] * decay, xdt)
    to_end = jnp.exp(a_cum[:, :, -1:] - a_cum)
    states = jnp.einsum('bclgn,bclgrp->bcgrpn', br, xdt * to_end[..., None])
    chunk_decay = jnp.exp(a_cum[:, :, -1])

    def step(h, inp):
        s_c, d_c = inp
        return h * d_c[..., None, None] + s_c, h

    h_final, h_prev = lax.scan(step, h0.astype(f32).reshape(bsz, ng, rep, hp, ns),
                               (jnp.moveaxis(states, 1, 0), jnp.moveaxis(chunk_decay, 1, 0)))
    h_prev = jnp.moveaxis(h_prev, 0, 1)
    y_off = jnp.einsum('bclgn,bcgrpn->bclgrp', cr, h_prev) * jnp.exp(a_cum)[..., None]
    y = (y_diag + y_off).reshape(bsz, length, nh, hp)
    return y, h_final.reshape(bsz, nh, hp, ns)


def _sb_block(q, k, v, q_pos, k_pos, bias):
    z = jnp.einsum('bqhd,bkhd->bhqk', q, k).astype(jnp.float32) * SB_SCALE \
        + bias.astype(jnp.float32)[None, :, None, None]
    causal = k_pos[None, :] < q_pos[:, None]
    log_beta = jax.nn.log_sigmoid(z)
    log_rest = jnp.where(causal, log_beta - z, 0.0)
    between = lax.cumsum(log_rest, axis=3, reverse=True) - log_rest
    w = jnp.where(causal, jnp.exp(log_beta + between), 0.0)
    return jnp.einsum('bhqk,bkhd->bqhd', w.astype(v.dtype), v)


def sb_attention(q, k, v, past, bias):
    bsz, length, nh, hd = q.shape
    qb = Q_BLOCK if length % Q_BLOCK == 0 else length
    nb = length // qb
    k_pos = jnp.arange(k.shape[1])
    q_blocks = jnp.moveaxis(q.reshape(bsz, nb, qb, nh, hd), 1, 0)
    q_pos = (past + jnp.arange(length)).reshape(nb, qb)
    out = lax.map(lambda a: _sb_block(a[0], k, v, a[1], k_pos, bias), (q_blocks, q_pos))
    return jnp.moveaxis(out, 0, 1).reshape(bsz, length, nh, hd)


def even_mixer(h, w_in, w_out, conv_w, conv_b, dt_bias, a_log, d_skip, norm_g, sb_bias, h0, conv_buf, k_past, v_past):
    f32 = jnp.float32
    bsz, length, _ = h.shape
    z, xbc, dt_raw, q, k, v = jnp.split(h @ w_in, SPLIT_EVEN, axis=-1)
    xbc, conv_new = causal_dwconv(xbc, conv_buf, conv_w)
    xbc = jax.nn.silu(xbc + conv_b)
    xs, bmat, cmat = jnp.split(xbc, (D_SSM, D_SSM + SSM_GROUPS * D_STATE), axis=-1)
    xs = xs.reshape(bsz, length, SSM_HEADS, SSM_HEAD_DIM)
    bmat = bmat.reshape(bsz, length, SSM_GROUPS, D_STATE)
    cmat = cmat.reshape(bsz, length, SSM_GROUPS, D_STATE)
    dt = jax.nn.softplus(dt_raw.astype(f32) + dt_bias.astype(f32))
    a = -jnp.exp(a_log.astype(f32))
    chunk = SSD_CHUNK if length % SSD_CHUNK == 0 else length
    y, h_new = ssd_scan(xs, dt, a, bmat, cmat, h0, chunk)
    y = y + xs.astype(f32) * d_skip.astype(f32)[:, None]
    y = y.reshape(bsz, length, D_SSM) * jax.nn.silu(z.astype(f32))
    y = group_rmsnorm(y, norm_g).astype(h.dtype)
    q = q.reshape(bsz, length, ATT_HEADS, ATT_HEAD_DIM)
    k = k.reshape(bsz, length, ATT_HEADS, ATT_HEAD_DIM)
    v = v.reshape(bsz, length, ATT_HEADS, ATT_HEAD_DIM)
    k_all = jnp.concatenate([k_past.astype(k.dtype), k], axis=1)
    v_all = jnp.concatenate([v_past.astype(v.dtype), v], axis=1)
    attn = sb_attention(q, k_all, v_all, k_past.shape[1], sb_bias)
    out = jnp.concatenate([y, attn.reshape(bsz, length, D_ATT)], axis=-1) @ w_out
    return out, h_new, conv_new, k, v


def odd_mixer(h, w_in, conv_w, w_out, conv_buf):
    b_gate, c_gate, u = jnp.split(h @ w_in, 3, axis=-1)
    conv, buf_new = causal_dwconv(c_gate * u, conv_buf, conv_w)
    return (b_gate * conv) @ w_out, buf_new


def hier_moe(h, w_group, b_group, w_router, b_router, w_gate, w_up, w_down):
    f32 = jnp.float32
    shp = h.shape
    t = h.reshape(-1, shp[-1])
    g_prob = jax.nn.softmax((t @ w_group + b_group).astype(f32), axis=-1)
    g_idx = jnp.argmax(g_prob, axis=-1)
    g_top = jnp.max(g_prob, axis=-1)
    e_logits = (t @ w_router + b_router).astype(f32).reshape(-1, N_EXPERT_GROUPS, EXPERTS_PER_GROUP)
    e_sel = jnp.einsum('tge,tg->te', e_logits, jax.nn.one_hot(g_idx, N_EXPERT_GROUPS, dtype=f32))
    top_p, top_i = lax.top_k(jax.nn.softmax(e_sel, axis=-1), TOP_K)
    wts = top_p / jnp.sum(top_p, axis=-1, keepdims=True) * g_top[:, None]
    expert_id = g_idx[:, None] * EXPERTS_PER_GROUP + top_i
    gates = jnp.einsum('tk,tke->te', wts, jax.nn.one_hot(expert_id, N_EXPERTS, dtype=f32))
    hg = jnp.einsum('td,edf->tef', t, w_gate)
    hu = jnp.einsum('td,edf->tef', t, w_up)
    hid = jax.nn.silu(hg) * hu * gates[:, :, None].astype(hg.dtype)
    out = jnp.einsum('tef,efd->td', hid, w_down)
    return out.astype(h.dtype).reshape(shp)


def setup_inputs(seed: int = 0) -> dict:
    key = jax.random.key(seed)
    ks = iter(jax.random.split(key, 40))
    f32 = jnp.float32

    def nrm(shape, scale=1.0):
        return jax.random.normal(next(ks), shape, f32) * scale

    def gain(shape):
        return 1.0 + nrm(shape, 0.02)

    n_pages = PAST_LEN // PAGE_SIZE
    n_used = DEC_BATCH * n_pages
    n_pool = n_used + n_used // 4
    d = D_MODEL
    x_prompt = nrm((BATCH, SEQ, d))
    x_sample = nrm((DEC_BATCH, DEC_SEQ, d))
    c_prompt = nrm((BATCH, d))
    c_sample = nrm((DEC_BATCH, d))
    state_ssm = nrm((N_EVEN, DEC_BATCH, SSM_HEADS, SSM_HEAD_DIM, D_STATE), 0.5)
    state_ssd_conv = nrm((N_EVEN, DEC_BATCH, SSD_CONV - 1, CONV_DIM))
    cache_k = nrm((N_EVEN, n_pool, PAGE_SIZE, ATT_HEADS, ATT_HEAD_DIM))
    cache_v = nrm((N_EVEN, n_pool, PAGE_SIZE, ATT_HEADS, ATT_HEAD_DIM))
    page_table = jax.random.permutation(next(ks), n_pool)[:n_used].reshape(DEC_BATCH, n_pages).astype(jnp.int32)
    state_short_conv = nrm((N_ODD, DEC_BATCH, SC_WIDTH - 1, D_SC))
    ada_w = nrm((DEPTH, d, 6 * d), 0.5 * d ** -0.5)
    ada_b = nrm((DEPTH, 6 * d), 0.02)
    ln_mix_g = gain((DEPTH, d))
    ln_ffn_g = gain((DEPTH, d))
    mix_w_in = nrm((N_EVEN, d, IN_EVEN), d ** -0.5)
    mix_w_out = nrm((N_EVEN, D_SSM + D_ATT, d), (D_SSM + D_ATT) ** -0.5)
    ssd_conv_w = nrm((N_EVEN, SSD_CONV, CONV_DIM), SSD_CONV ** -0.5)
    ssd_conv_b = nrm((N_EVEN, CONV_DIM), 0.02)
    dt0 = jnp.exp(jax.random.uniform(next(ks), (N_EVEN, SSM_HEADS), f32, np.log(1e-3), np.log(1e-1)))
    ssd_dt_bias = dt0 + jnp.log(-jnp.expm1(-dt0))
    ssd_a_log = jnp.log(jax.random.uniform(next(ks), (N_EVEN, SSM_HEADS), f32, 1.0, 16.0))
    ssd_d = 1.0 + nrm((N_EVEN, SSM_HEADS), 0.1)
    ssd_norm_g = gain((N_EVEN, D_SSM))
    sb_bias = SB_BIAS_INIT + nrm((N_EVEN, ATT_HEADS), 0.5)
    sc_w_in = nrm((N_ODD, d, 3 * D_SC), d ** -0.5)
    sc_conv_w = nrm((N_ODD, SC_WIDTH, D_SC), SC_WIDTH ** -0.5)
    sc_w_out = nrm((N_ODD, D_SC, d), D_SC ** -0.5)
    moe_w_group = nrm((DEPTH, d, N_EXPERT_GROUPS), d ** -0.5)
    moe_b_group = nrm((DEPTH, N_EXPERT_GROUPS), 0.01)
    moe_w_router = nrm((DEPTH, d, N_EXPERTS), d ** -0.5)
    moe_b_router = nrm((DEPTH, N_EXPERTS), 0.01)
    moe_w_gate = nrm((DEPTH, N_EXPERTS, d, D_EXPERT), d ** -0.5)
    moe_w_up = nrm((DEPTH, N_EXPERTS, d, D_EXPERT), d ** -0.5)
    moe_w_down = nrm((DEPTH, N_EXPERTS, D_EXPERT, d), D_EXPERT ** -0.5)
    final_g = gain((d,))
    return {"x_prompt": x_prompt, "x_sample": x_sample, "c_prompt": c_prompt, "c_sample": c_sample,
            "state_ssm": state_ssm, "state_ssd_conv": state_ssd_conv, "cache_k": cache_k, "cache_v": cache_v,
            "page_table": page_table, "state_short_conv": state_short_conv,
            "ada_w": ada_w, "ada_b": ada_b, "ln_mix_g": ln_mix_g, "ln_ffn_g": ln_ffn_g,
            "mix_w_in": mix_w_in, "mix_w_out": mix_w_out, "ssd_conv_w": ssd_conv_w, "ssd_conv_b": ssd_conv_b,
            "ssd_dt_bias": ssd_dt_bias, "ssd_a_log": ssd_a_log, "ssd_d": ssd_d, "ssd_norm_g": ssd_norm_g,
            "sb_bias": sb_bias,
            "sc_w_in": sc_w_in, "sc_conv_w": sc_conv_w, "sc_w_out": sc_w_out,
            "moe_w_group": moe_w_group, "moe_b_group": moe_b_group, "moe_w_router": moe_w_router,
            "moe_b_router": moe_b_router, "moe_w_gate": moe_w_gate, "moe_w_up": moe_w_up,
            "moe_w_down": moe_w_down, "final_g": final_g}


def reference(x_prompt, x_sample, c_prompt, c_sample, state_ssm, state_ssd_conv, cache_k, cache_v, page_table,
              state_short_conv, ada_w, ada_b, ln_mix_g, ln_ffn_g, mix_w_in, mix_w_out, ssd_conv_w, ssd_conv_b,
              ssd_dt_bias, ssd_a_log, ssd_d, ssd_norm_g, sb_bias, sc_w_in, sc_conv_w, sc_w_out, moe_w_group,
              moe_b_group, moe_w_router, moe_b_router, moe_w_gate, moe_w_up, moe_w_down, final_g):
    xp, xs = x_prompt, x_sample
    bp, bs = xp.shape[0], xs.shape[0]
    past_len = page_table.shape[1] * cache_k.shape[2]
    ssm_p, ssm_s, cv_p, cv_s, kp, ksm, vp, vsm, sc_p, sc_s = [], [], [], [], [], [], [], [], [], []
    for layer in range(DEPTH):
        mp = ada_modulation(c_prompt, ada_w[layer], ada_b[layer])
        ms = ada_modulation(c_sample, ada_w[layer], ada_b[layer])
        hp = modulate(rmsnorm(xp, ln_mix_g[layer]), mp[0], mp[1])
        hs = modulate(rmsnorm(xs, ln_mix_g[layer]), ms[0], ms[1])
        if layer % 2 == 0:
            i = layer // 2
            prm = (mix_w_in[i], mix_w_out[i], ssd_conv_w[i], ssd_conv_b[i], ssd_dt_bias[i], ssd_a_log[i],
                   ssd_d[i], ssd_norm_g[i], sb_bias[i])
            zero_h = jnp.zeros((bp, SSM_HEADS, SSM_HEAD_DIM, D_STATE), jnp.float32)
            zero_buf = jnp.zeros((bp, SSD_CONV - 1, CONV_DIM), xp.dtype)
            no_kv = jnp.zeros((bp, 0, ATT_HEADS, ATT_HEAD_DIM), xp.dtype)
            op, h_p, b_p, k_p, v_p = even_mixer(hp, *prm, zero_h, zero_buf, no_kv, no_kv)
            k_past = jnp.take(cache_k[i], page_table, axis=0).reshape(bs, past_len, ATT_HEADS, ATT_HEAD_DIM)
            v_past = jnp.take(cache_v[i], page_table, axis=0).reshape(bs, past_len, ATT_HEADS, ATT_HEAD_DIM)
            osm, h_s, b_s, k_s, v_s = even_mixer(hs, *prm, state_ssm[i], state_ssd_conv[i], k_past, v_past)
            ssm_p.append(h_p); ssm_s.append(h_s); cv_p.append(b_p); cv_s.append(b_s)
            kp.append(k_p); ksm.append(k_s); vp.append(v_p); vsm.append(v_s)
        else:
            i = layer // 2
            zero_buf = jnp.zeros((bp, SC_WIDTH - 1, D_SC), xp.dtype)
            op, b_p = odd_mixer(hp, sc_w_in[i], sc_conv_w[i], sc_w_out[i], zero_buf)
            osm, b_s = odd_mixer(hs, sc_w_in[i], sc_conv_w[i], sc_w_out[i], state_short_conv[i])
            sc_p.append(b_p); sc_s.append(b_s)
        xp = xp + mp[2] * op.astype(xp.dtype)
        xs = xs + ms[2] * osm.astype(xs.dtype)
        moe_prm = (moe_w_group[layer], moe_b_group[layer], moe_w_router[layer], moe_b_router[layer],
                   moe_w_gate[layer], moe_w_up[layer], moe_w_down[layer])
        xp = xp + mp[5] * hier_moe(modulate(rmsnorm(xp, ln_ffn_g[layer]), mp[3], mp[4]), *moe_prm)
        xs = xs + ms[5] * hier_moe(modulate(rmsnorm(xs, ln_ffn_g[layer]), ms[3], ms[4]), *moe_prm)
    y_prompt = rmsnorm(xp, final_g)
    y_sample = rmsnorm(xs, final_g)
    return (y_prompt, y_sample, jnp.stack(ssm_p), jnp.stack(ssm_s), jnp.stack(cv_p), jnp.stack(cv_s),
            jnp.stack(kp), jnp.stack(ksm), jnp.stack(vp), jnp.stack(vsm), jnp.stack(sc_p), jnp.stack(sc_s))
```

```python
import functools

import jax
import jax.numpy as jnp
from jax import lax
from jax.experimental import pallas as pl
from jax.experimental.pallas import tpu as pltpu

F32 = jnp.float32
BF16 = jnp.bfloat16
I32 = jnp.int32

SSM_HEADS = 16
SSM_HEAD_DIM = 64
D_SSM = SSM_HEADS * SSM_HEAD_DIM
SSM_GROUPS = 2
HEADS_PER_GROUP = SSM_HEADS // SSM_GROUPS
D_STATE = 128
SSD_CONV = 4
SSD_CHUNK = 128
CONV_DIM = D_SSM + 2 * SSM_GROUPS * D_STATE
ATT_HEADS = 16
ATT_HEAD_DIM = 64
D_ATT = ATT_HEADS * ATT_HEAD_DIM
SB_SCALE = ATT_HEAD_DIM ** -0.5
SC_WIDTH = 3
N_EXPERT_GROUPS = 4
EXPERTS_PER_GROUP = 8
N_EXPERTS = N_EXPERT_GROUPS * EXPERTS_PER_GROUP
TOP_K = 2
D_EXPERT = 256
RMS_EPS = 1e-6

LANES = 128
SUBLANES = 8
VMEM_BYTES_V7X = 64 * 1024 * 1024
VMEM_LIMIT_CAP = VMEM_BYTES_V7X - 8 * 1024 * 1024

TOKEN_TILE = 256
ATT_TILE = 256
ATT_HEAD_BLOCK = 4
MOE_TILE = 256
ROW_DMA_TILE = 512
ROUTE_LANE_E = 0
ROUTE_LANE_R = 2
ROUTE_LANE_W = 4
ROUTER_LANE0 = N_EXPERT_GROUPS


def _vmem_limit(nbytes):
    return int(min(max(2 * nbytes, 32 * 1024 * 1024), VMEM_LIMIT_CAP))


def _params(nbytes, n_axes):
    return pltpu.CompilerParams(dimension_semantics=("arbitrary",) * n_axes,
                                vmem_limit_bytes=_vmem_limit(nbytes))


def _dot(a, b):
    return jnp.dot(a, b, preferred_element_type=F32)


def _dot_nt(a, b):
    return lax.dot_general(a, b, (((1,), (1,)), ((), ())), preferred_element_type=F32)


def _dot_tn(a, b):
    return lax.dot_general(a, b, (((0,), (0,)), ((), ())), preferred_element_type=F32)


def _split2(x):
    hi = x.astype(BF16)
    lo = (x - hi.astype(F32)).astype(BF16)
    return hi, lo


def _split3(x):
    hi = x.astype(BF16)
    r = x - hi.astype(F32)
    mid = r.astype(BF16)
    lo = (r - mid.astype(F32)).astype(BF16)
    return hi, mid, lo


def _dot3(a, b):
    ah, al = _split2(a)
    bh, bl = _split2(b)
    return _dot(ah, bh) + (_dot(al, bh) + _dot(ah, bl))


def _silu(x):
    return x / (1.0 + jnp.exp(-x))


def _softplus(x):
    return jnp.maximum(x, 0.0) + jnp.log(1.0 + jnp.exp(-jnp.abs(x)))


def _rmsnorm(x, g):
    return x * lax.rsqrt(jnp.mean(x * x, axis=-1, keepdims=True) + RMS_EPS) * g


def _row(ref):
    v = ref[...]
    return v[0] if v.ndim == 3 else v


def _adaln_kernel(c_ref, w_ref, b_ref, o_ref):
    o_ref[...] = _dot3(_silu(c_ref[...]), w_ref[...]) + b_ref[...]


def _adaln(c_all, ada_w, ada_b):
    depth, d, n = ada_w.shape
    rows = c_all.shape[0]
    tn = 1024
    return pl.pallas_call(
        _adaln_kernel,
        out_shape=jax.ShapeDtypeStruct((depth, rows, n), F32),
        grid=(depth, n // tn),
        in_specs=[pl.BlockSpec((rows, d), lambda l, j: (0, 0)),
                  pl.BlockSpec((None, d, tn), lambda l, j: (l, 0, j)),
                  pl.BlockSpec((None, 1, tn), lambda l, j: (l, 0, j))],
        out_specs=pl.BlockSpec((None, rows, tn), lambda l, j: (l, 0, j)),
        compiler_params=_params(2 * d * tn * 4 + 4 * rows * (d + tn) * 4, 2),
        name="adaln",
    )(c_all, ada_w, ada_b.reshape(depth, 1, n))


class _Mod:
    def __init__(self, arr, seq_len, d):
        self.arr = arr
        self.seq_len = seq_len
        self.d = d

    def spec(self, which, tm):
        d = self.d
        if self.arr.ndim == 3:
            tiles_per_seq = self.seq_len // tm
            return pl.BlockSpec((1, 1, d), lambda i, *_: (i // tiles_per_seq, 0, which))
        return pl.BlockSpec((tm, d), lambda i, *_: (i, which))


def _pre_mix_kernel(x_ref, g_ref, shift_ref, scale_ref, w_ref, *out_refs, splits):
    h = (_rmsnorm(x_ref[...], g_ref[...]) * (1.0 + _row(scale_ref)) + _row(shift_ref)).astype(BF16)
    k = 0
    for start, width, dtypes in splits:
        acc = _dot(h, w_ref[:, start:start + width])
        for dt in dtypes:
            out_refs[k][...] = acc.astype(dt)
            k += 1


def _pre_mix(x, g, mod, w_bf, splits, name):
    t, d = x.shape
    n = w_bf.shape[1]
    tm = TOKEN_TILE
    out_shape, out_specs, out_bytes = [], [], 0
    for _, width, dtypes in splits:
        for dt in dtypes:
            out_shape.append(jax.ShapeDtypeStruct((t, width), dt))
            out_specs.append(pl.BlockSpec((tm, width), lambda i: (i, 0)))
            out_bytes += tm * width * jnp.dtype(dt).itemsize
    return pl.pallas_call(
        functools.partial(_pre_mix_kernel, splits=splits),
        out_shape=out_shape,
        grid=(t // tm,),
        in_specs=[pl.BlockSpec((tm, d), lambda i: (i, 0)),
                  pl.BlockSpec((1, d), lambda i: (0, 0)),
                  mod.spec(0, tm), mod.spec(1, tm),
                  pl.BlockSpec((d, n), lambda i: (0, 0))],
        out_specs=out_specs,
        compiler_params=_params(2 * d * n * 2 + 2 * tm * d * 4 + 2 * out_bytes + tm * n * 4, 1),
        name=name,
    )(x, g.reshape(1, d), mod.arr, mod.arr, w_bf)


def _ssd_kernel(*refs, chunk, n_real, has_init):
    if has_init:
        (xbc_ref, z_ref, dt_ref, tail0_ref, h0_ref, cw_ref, cb_ref, dtb_ref, alog_ref, dsk_ref, ng_ref,
         y_ref, hout_ref, cvout_ref, buf_ref, state_ref, ybuf_ref) = refs
    else:
        (xbc_ref, z_ref, dt_ref, cw_ref, cb_ref, dtb_ref, alog_ref, dsk_ref, ng_ref,
         y_ref, hout_ref, cvout_ref, buf_ref, state_ref, ybuf_ref) = refs
    c = pl.program_id(1)
    last = pl.num_programs(1) - 1
    L = chunk
    T0 = SUBLANES

    @pl.when(c == 0)
    def _():
        if has_init:
            buf_ref[0:T0, :] = tail0_ref[0]
            state_ref[...] = h0_ref[0]
        else:
            buf_ref[0:T0, :] = jnp.zeros((T0, CONV_DIM), F32)
            state_ref[...] = jnp.zeros_like(state_ref)

    xbc = xbc_ref[...]
    buf_ref[T0:T0 + L, :] = xbc
    conv = xbc * cw_ref[3:4, :]
    for j in range(SSD_CONV - 1):
        conv = conv + buf_ref[pl.ds(T0 - (SSD_CONV - 1) + j, L), :] * cw_ref[j:j + 1, :]
    xc = _silu(conv + cb_ref[...])
    xs = xc[:, :D_SSM]
    bmat = xc[:, D_SSM:D_SSM + SSM_GROUPS * D_STATE].astype(BF16)
    cmat = xc[:, D_SSM + SSM_GROUPS * D_STATE:].astype(BF16)

    @pl.when(c == last)
    def _():
        cvout_ref[0] = buf_ref[pl.ds(T0 + n_real - (SSD_CONV - 1), SSD_CONV - 1), :]

    if n_real == L:
        buf_ref[0:T0, :] = buf_ref[L:L + T0, :]

    dt = _softplus(dt_ref[...] + dtb_ref[...])
    rows = lax.broadcasted_iota(I32, (L, L), 0)
    cols = lax.broadcasted_iota(I32, (L, L), 1)
    if n_real < L:
        dt = jnp.where(lax.broadcasted_iota(I32, dt.shape, 0) < n_real, dt, 0.0)
    a = -jnp.exp(alog_ref[...])
    tri = cols <= rows
    tri_bf = jnp.where(tri, 1.0, 0.0).astype(BF16)
    da_hi, da_mid, da_lo = _split3(dt * a)
    acum = _dot(tri_bf, da_hi) + (_dot(tri_bf, da_mid) + _dot(tri_bf, da_lo))
    acum_t = acum.T
    a_last = acum[L - 1:L, :]
    to_end = jnp.exp(a_last - acum)
    e_acum = jnp.exp(acum)
    chunk_decay = jnp.exp(a_last)

    cb = [_dot_nt(cmat[:, g * D_STATE:(g + 1) * D_STATE], bmat[:, g * D_STATE:(g + 1) * D_STATE])
          for g in range(SSM_GROUPS)]
    for h in range(SSM_HEADS):
        g = h // HEADS_PER_GROUP
        hs = slice(h * SSM_HEAD_DIM, (h + 1) * SSM_HEAD_DIM)
        gs = slice(g * D_STATE, (g + 1) * D_STATE)
        seg = acum[:, h:h + 1] - acum_t[h:h + 1, :]
        decay = jnp.exp(jnp.where(tri, seg, -jnp.inf))
        m = (cb[g] * decay).astype(BF16)
        xh = xs[:, hs]
        xdt = xh * dt[:, h:h + 1]
        y_diag = _dot(m, xdt.astype(BF16))
        st = state_ref[hs, :]
        y_off = _dot_nt(cmat[:, gs], st.astype(BF16)) * e_acum[:, h:h + 1]
        s_new = _dot_tn((xdt * to_end[:, h:h + 1]).astype(BF16), bmat[:, gs])
        state_ref[hs, :] = st * chunk_decay[:, h:h + 1] + s_new
        ybuf_ref[:, hs] = y_diag + y_off + xh * dsk_ref[:, hs]

    y = ybuf_ref[...] * _silu(z_ref[...])
    half = D_SSM // SSM_GROUPS
    for g in range(SSM_GROUPS):
        yg = y[:, g * half:(g + 1) * half]
        y_ref[:, g * half:(g + 1) * half] = (
            yg * lax.rsqrt(jnp.mean(yg * yg, axis=-1, keepdims=True) + RMS_EPS) * ng_ref[:, g * half:(g + 1) * half]
        ).astype(y_ref.dtype)

    @pl.when(c == last)
    def _():
        hout_ref[0] = state_ref[...]


def _ssd(xbc, z, dt, prm, n_seq, chunk, n_real, tail0=None, h0=None, name="ssd"):
    t = xbc.shape[0]
    nc = t // (n_seq * chunk)
    has_init = tail0 is not None
    cw, cb, dtb, alog, dsk, ng = prm
    row_spec = lambda w: pl.BlockSpec((chunk, w), lambda b, c: (b * nc + c, 0))
    const = lambda shp: pl.BlockSpec(shp, lambda b, c: (0,) * len(shp))
    in_specs = [row_spec(CONV_DIM), row_spec(D_SSM), row_spec(LANES)]
    args = [xbc, z, dt]
    if has_init:
        in_specs += [pl.BlockSpec((1, SUBLANES, CONV_DIM), lambda b, c: (b, 0, 0)),
                     pl.BlockSpec((1, D_SSM, D_STATE), lambda b, c: (b, 0, 0))]
        args += [tail0, h0]
    in_specs += [const((SSD_CONV, CONV_DIM)), const((1, CONV_DIM)), const((1, LANES)), const((1, LANES)),
                 const((1, D_SSM)), const((1, D_SSM))]
    args += [cw, cb, dtb, alog, dsk, ng]
    scratch_bytes = ((chunk + 2 * SUBLANES) * CONV_DIM + D_SSM * D_STATE + chunk * D_SSM) * 4
    block_bytes = chunk * (CONV_DIM + D_SSM + LANES) * 4 + chunk * D_SSM * 2 + 2 * D_SSM * D_STATE * 4
    return pl.pallas_call(
        functools.partial(_ssd_kernel, chunk=chunk, n_real=n_real, has_init=has_init),
        out_shape=[jax.ShapeDtypeStruct((t, D_SSM), BF16),
                   jax.ShapeDtypeStruct((n_seq, D_SSM, D_STATE), F32),
                   jax.ShapeDtypeStruct((n_seq, SSD_CONV - 1, CONV_DIM), F32)],
        grid=(n_seq, nc),
        in_specs=in_specs,
        out_specs=[row_spec(D_SSM),
                   pl.BlockSpec((1, D_SSM, D_STATE), lambda b, c: (b, 0, 0)),
                   pl.BlockSpec((1, SSD_CONV - 1, CONV_DIM), lambda b, c: (b, 0, 0))],
        scratch_shapes=[pltpu.VMEM((chunk + 2 * SUBLANES, CONV_DIM), F32),
                        pltpu.VMEM((D_SSM, D_STATE), F32),
                        pltpu.VMEM((chunk, D_SSM), F32)],
        compiler_params=_params(2 * block_bytes + scratch_bytes + 8 * chunk * CONV_DIM * 4, 2),
        name=name,
    )(*args)


def _sb_tile(s, carry, upper, mask):
    m = s.shape[0]
    sp = _softplus(s)
    log_rest = -sp
    log_beta = s - sp
    if mask is not None:
        log_rest = jnp.where(mask, log_rest, 0.0)
    hi, lo = _split2(log_rest)
    r = _dot(jnp.concatenate([hi, lo], axis=0), upper)
    between = r[:m] + r[m:] + carry
    w = jnp.exp(log_beta + between)
    if mask is not None:
        w = jnp.where(mask, w, 0.0)
    return w, between[:, 0:1] + log_rest[:, 0:1]


def _strict_upper(n):
    j = lax.broadcasted_iota(I32, (n, n), 0)
    k = lax.broadcasted_iota(I32, (n, n), 1)
    return jnp.where(j > k, 1.0, 0.0).astype(BF16)


def _attn_prompt_kernel(bias_ref, q_ref, k_ref, v_ref, o_ref, acc_ref):
    hq = pl.program_id(1)
    qi = pl.program_id(2)
    tq, width = q_ref.shape
    tk = ATT_TILE
    lane_head = lax.broadcasted_iota(I32, (1, width), 1) // ATT_HEAD_DIM
    upper = _strict_upper(tk)
    causal = lax.broadcasted_iota(I32, (tq, tk), 1) < lax.broadcasted_iota(I32, (tq, tk), 0)
    q = q_ref[...]
    out = jnp.zeros((tq, width), F32)
    for h in range(ATT_HEAD_BLOCK):
        qh = jnp.where(lane_head == h, q, jnp.zeros_like(q))
        bias = bias_ref[hq * ATT_HEAD_BLOCK + h]

        def tile(j, carry, mask, qh=qh, bias=bias):
            start = pl.multiple_of(j * tk, tk)
            s = _dot_nt(qh, k_ref[pl.ds(start, tk), :]) + bias
            w, carry = _sb_tile(s, carry, upper, mask)
            acc_ref[...] += _dot(w.astype(BF16), v_ref[pl.ds(start, tk), :])
            return carry

        acc_ref[...] = jnp.zeros_like(acc_ref)
        carry = tile(qi, jnp.zeros((tq, 1), F32), causal)
        lax.fori_loop(0, qi, lambda i, cr: tile(qi - 1 - i, cr, None), carry)
        out = jnp.where(lane_head == h, acc_ref[...], out)
    o_ref[...] = out.astype(o_ref.dtype)


def _attn_prompt(q, k, v, bias, n_seq):
    t, d = q.shape
    s = t // n_seq
    tq = ATT_TILE
    nq = s // tq
    width = ATT_HEAD_BLOCK * ATT_HEAD_DIM
    return pl.pallas_call(
        _attn_prompt_kernel,
        out_shape=jax.ShapeDtypeStruct((t, d), BF16),
        grid_spec=pltpu.PrefetchScalarGridSpec(
            num_scalar_prefetch=1,
            grid=(n_seq, d // width, nq),
            in_specs=[pl.BlockSpec((tq, width), lambda b, hq, qi, *_: (b * nq + qi, hq)),
                      pl.BlockSpec((s, width), lambda b, hq, qi, *_: (b, hq)),
                      pl.BlockSpec((s, width), lambda b, hq, qi, *_: (b, hq))],
            out_specs=pl.BlockSpec((tq, width), lambda b, hq, qi, *_: (b * nq + qi, hq)),
            scratch_shapes=[pltpu.VMEM((tq, width), F32)]),
        compiler_params=_params(4 * s * width * 2 + 4 * tq * width * 2 + 16 * tq * ATT_TILE * 4, 3),
        name="sb_attn_prompt",
    )(bias, q, k, v)


def _attn_sample_kernel(pt_ref, bias_ref, q_ref, kn_ref, vn_ref, *refs, n_pages, page, n_new):
    k_refs = refs[:n_pages]
    v_refs = refs[n_pages:2 * n_pages]
    o_ref = refs[2 * n_pages]
    d = q_ref.shape[-1]
    n_rows = n_new * ATT_HEADS
    row_head = lax.broadcasted_iota(I32, (n_rows, 1), 0) % ATT_HEADS
    lane_head = lax.broadcasted_iota(I32, (1, d), 1) // ATT_HEAD_DIM
    own = lane_head == row_head
    q = q_ref[0]
    qx = jnp.broadcast_to(q[:, None, :], (n_new, ATT_HEADS, d)).reshape(n_rows, d)
    qx = jnp.where(own, qx, jnp.zeros_like(qx))
    bias = jnp.zeros((n_rows, 1), F32)
    for h in range(ATT_HEADS):
        bias = jnp.where(row_head == h, bias_ref[h], bias)
    upper = _strict_upper(page)

    def tile(kt, vt, carry, mask):
        s = _dot_nt(qx, kt) + bias
        w, carry = _sb_tile(s, carry, upper, mask)
        return _dot(w.astype(BF16), vt), carry

    pad = jnp.zeros((page - kn_ref.shape[1], d), F32)
    kt = jnp.concatenate([kn_ref[0], pad], axis=0).astype(BF16)
    vt = jnp.concatenate([vn_ref[0], pad], axis=0).astype(BF16)
    row_t = lax.broadcasted_iota(I32, (n_rows, page), 0) // ATT_HEADS
    new_mask = lax.broadcasted_iota(I32, (n_rows, page), 1) < row_t
    acc, carry = tile(kt, vt, jnp.zeros((n_rows, 1), F32), new_mask)
    for p in range(n_pages - 1, -1, -1):
        part, carry = tile(k_refs[p][0].astype(BF16), v_refs[p][0].astype(BF16), carry, None)
        acc = acc + part
    acc = jnp.where(own, acc, 0.0)
    o_ref[0] = jnp.sum(acc.reshape(n_new, ATT_HEADS, d), axis=1).astype(o_ref.dtype)


def _attn_sample(q, k_new, v_new, cache_k, cache_v, page_table, bias):
    bs, n_new, d = q.shape
    n_pages = page_table.shape[1]
    page = cache_k.shape[1]
    page_spec = lambda p: pl.BlockSpec((1, page, d), lambda b, pt, bias_: (pt[b, p], 0, 0))
    seq_spec = lambda rows: pl.BlockSpec((1, rows, d), lambda b, pt, bias_: (b, 0, 0))
    return pl.pallas_call(
        functools.partial(_attn_sample_kernel, n_pages=n_pages, page=page, n_new=n_new),
        out_shape=jax.ShapeDtypeStruct((bs, n_new, d), BF16),
        grid_spec=pltpu.PrefetchScalarGridSpec(
            num_scalar_prefetch=2,
            grid=(bs,),
            in_specs=[seq_spec(n_new), seq_spec(k_new.shape[1]), seq_spec(v_new.shape[1])]
            + [page_spec(p) for p in range(n_pages)] * 2,
            out_specs=seq_spec(n_new)),
        compiler_params=_params(2 * 2 * n_pages * page * d * 4 + 8 * page * d * 4, 1),
        name="sb_attn_sample",
    )(page_table, bias, q, k_new, v_new, *([cache_k] * n_pages), *([cache_v] * n_pages))


def _sc_prompt_kernel(b_ref, c_ref, u_ref, w_ref, a_ref, st_ref, buf_ref):
    L = c_ref.shape[0]
    T0 = SUBLANES
    cu = c_ref[...] * u_ref[...]
    buf_ref[0:T0, :] = jnp.zeros((T0, cu.shape[1]), F32)
    buf_ref[T0:T0 + L, :] = cu
    conv = cu * w_ref[SC_WIDTH - 1:SC_WIDTH, :]
    for j in range(SC_WIDTH - 1):
        conv = conv + buf_ref[pl.ds(T0 - (SC_WIDTH - 1) + j, L), :] * w_ref[j:j + 1, :]
    a_ref[...] = (b_ref[...] * conv).astype(a_ref.dtype)
    st_ref[0] = buf_ref[pl.ds(T0 + L - (SC_WIDTH - 1), SC_WIDTH - 1), :]


def _sc_prompt(bg, cg, u, conv_w, n_seq):
    t, d = cg.shape
    s = t // n_seq
    wl = 256
    spec = pl.BlockSpec((s, wl), lambda b, j: (b, j))
    return pl.pallas_call(
        _sc_prompt_kernel,
        out_shape=[jax.ShapeDtypeStruct((t, d), BF16),
                   jax.ShapeDtypeStruct((n_seq, SC_WIDTH - 1, d), F32)],
        grid=(n_seq, d // wl),
        in_specs=[spec, spec, spec, pl.BlockSpec((SC_WIDTH, wl), lambda b, j: (0, j))],
        out_specs=[spec, pl.BlockSpec((1, SC_WIDTH - 1, wl), lambda b, j: (b, 0, j))],
        scratch_shapes=[pltpu.VMEM((s + SUBLANES, wl), F32)],
        compiler_params=_params(2 * 3 * s * wl * 4 + 2 * s * wl * 2 + 6 * s * wl * 4, 2),
        name="short_conv_prompt",
    )(bg, cg, u, conv_w)


def _sc_sample_kernel(b_ref, c_ref, u_ref, st_ref, w_ref, a_ref, sto_ref, *, n_new, d):
    up = [st_ref[:, j * d:(j + 1) * d] for j in range(SC_WIDTH - 1)]
    up += [c_ref[:, t * d:(t + 1) * d] * u_ref[:, t * d:(t + 1) * d] for t in range(n_new)]
    for t in range(n_new):
        conv = up[t] * w_ref[0:1, :]
        for j in range(1, SC_WIDTH):
            conv = conv + up[t + j] * w_ref[j:j + 1, :]
        a_ref[:, t * d:(t + 1) * d] = (b_ref[:, t * d:(t + 1) * d] * conv).astype(a_ref.dtype)
    for j in range(SC_WIDTH - 1):
        sto_ref[:, j * d:(j + 1) * d] = up[n_new + j]


def _sc_sample(bg, cg, u, state, conv_w, n_seq):
    t, d = cg.shape
    n_new = t // n_seq
    wide = lambda x: x.reshape(n_seq, n_new * d)
    a, st = pl.pallas_call(
        functools.partial(_sc_sample_kernel, n_new=n_new, d=d),
        out_shape=[jax.ShapeDtypeStruct((n_seq, n_new * d), BF16),
                   jax.ShapeDtypeStruct((n_seq, (SC_WIDTH - 1) * d), F32)],
        compiler_params=pltpu.CompilerParams(vmem_limit_bytes=_vmem_limit(16 * n_seq * n_new * d * 4)),
        name="short_conv_sample",
    )(wide(bg), wide(cg), wide(u), state.reshape(n_seq, (SC_WIDTH - 1) * d), conv_w)
    return a.reshape(t, d), st.reshape(n_seq, SC_WIDTH - 1, d)


def _post_mix_kernel(*refs, n_in):
    a_refs = refs[:n_in]
    w_refs = refs[n_in:2 * n_in]
    x_ref, gate_ref, g_ref, shift_ref, scale_ref, wr_ref, br_ref, xmid_ref, h_ref, logit_ref = refs[2 * n_in:]
    acc = _dot(a_refs[0][...], w_refs[0][...])
    for a_ref, w_ref in zip(a_refs[1:], w_refs[1:]):
        acc = acc + _dot(a_ref[...], w_ref[...])
    xm = x_ref[...] + _row(gate_ref) * acc
    xmid_ref[...] = xm
    h = _rmsnorm(xm, g_ref[...]) * (1.0 + _row(scale_ref)) + _row(shift_ref)
    h_ref[...] = h
    logit_ref[...] = _dot3(h, wr_ref[...]) + br_ref[...]


def _post_mix(a_list, w_list, x, g, mod, w_route, b_route, name):
    t, d = x.shape
    tm = TOKEN_TILE
    n_in = len(a_list)
    row = lambda w: pl.BlockSpec((tm, w), lambda i: (i, 0))
    const = lambda shp: pl.BlockSpec(shp, lambda i: (0, 0))
    w_bytes = sum(w.size * 2 for w in w_list)
    return pl.pallas_call(
        functools.partial(_post_mix_kernel, n_in=n_in),
        out_shape=[jax.ShapeDtypeStruct((t, d), F32), jax.ShapeDtypeStruct((t, d), F32),
                   jax.ShapeDtypeStruct((t, LANES), F32)],
        grid=(t // tm,),
        in_specs=[row(a.shape[1]) for a in a_list] + [const(w.shape) for w in w_list]
        + [row(d), mod.spec(2, tm), const((1, d)), mod.spec(3, tm), mod.spec(4, tm), const((d, LANES)), const((1, LANES))],
        out_specs=[row(d), row(d), row(LANES)],
        compiler_params=_params(2 * w_bytes + 2 * tm * d * (2 * n_in + 12) + 8 * tm * d * 4, 1),
        name=name,
    )(*a_list, *w_list, x, mod.arr, g.reshape(1, d), mod.arr, mod.arr, w_route, b_route)


def _route_kernel(lg_ref, route_ref, cnt_ref, acc_ref):
    i = pl.program_id(0)
    tm = lg_ref.shape[0]

    @pl.when(i == 0)
    def _():
        acc_ref[...] = jnp.zeros_like(acc_ref)

    lg = lg_ref[...]
    lane = lax.broadcasted_iota(I32, lg.shape, 1)
    neg = -jnp.inf
    gl = jnp.where(lane < N_EXPERT_GROUPS, lg, neg)
    gmax = jnp.max(gl, axis=-1, keepdims=True)
    gidx = jnp.min(jnp.where(gl == gmax, lane, LANES), axis=-1, keepdims=True)
    g_top = 1.0 / jnp.sum(jnp.exp(gl - gmax), axis=-1, keepdims=True)
    lo = ROUTER_LANE0 + gidx * EXPERTS_PER_GROUP
    el = jnp.where((lane >= lo) & (lane < lo + EXPERTS_PER_GROUP), lg, neg)
    m1 = jnp.max(el, axis=-1, keepdims=True)
    i1 = jnp.min(jnp.where(el == m1, lane, LANES), axis=-1, keepdims=True)
    el2 = jnp.where(lane == i1, neg, el)
    m2 = jnp.max(el2, axis=-1, keepdims=True)
    i2 = jnp.min(jnp.where(el2 == m2, lane, LANES), axis=-1, keepdims=True)
    p2 = jnp.exp(m2 - m1)
    w1 = g_top / (1.0 + p2)
    w2 = w1 * p2
    sel1 = lane == i1
    sel2 = lane == i2
    onehot = jnp.where(sel1 | sel2, 1.0, 0.0).astype(BF16)
    r = lax.broadcasted_iota(I32, (tm, tm), 0)
    c = lax.broadcasted_iota(I32, (tm, tm), 1)
    tri = jnp.where(c <= r, 1.0, 0.0).astype(BF16)
    cum = _dot(tri, onehot) + acc_ref[...]
    r1 = jnp.sum(jnp.where(sel1, cum, 0.0), axis=-1, keepdims=True) - 1.0
    r2 = jnp.sum(jnp.where(sel2, cum, 0.0), axis=-1, keepdims=True) - 1.0
    acc_ref[...] = cum[tm - 1:tm, :]
    cnt_ref[...] = cum[tm - 1:tm, :]
    e1 = (i1 - ROUTER_LANE0).astype(F32)
    e2 = (i2 - ROUTER_LANE0).astype(F32)
    rec = jnp.zeros(lg.shape, F32)
    for k, val in enumerate((e1, e2, r1, r2, w1, w2)):
        rec = jnp.where(lane == k, val, rec)
    route_ref[...] = rec


def _route(logits):
    t = logits.shape[0]
    tm = TOKEN_TILE
    return pl.pallas_call(
        _route_kernel,
        out_shape=[jax.ShapeDtypeStruct((t, LANES), F32), jax.ShapeDtypeStruct((1, LANES), F32)],
        grid=(t // tm,),
        in_specs=[pl.BlockSpec((tm, LANES), lambda i: (i, 0))],
        out_specs=[pl.BlockSpec((tm, LANES), lambda i: (i, 0)), pl.BlockSpec((1, LANES), lambda i: (0, 0))],
        scratch_shapes=[pltpu.VMEM((1, LANES), F32)],
        compiler_params=_params(64 * tm * LANES * 4, 1),
        name="moe_route",
    )(logits)


def _row_copy(src_ref, src_row, dst_ref, dst_row, sem):
    return pltpu.make_async_copy(src_ref.at[pl.ds(src_row, 1)], dst_ref.at[pl.ds(dst_row, 1)], sem)


def _dispatch_kernel(dest_ref, h_ref, xs_in_ref, xs_ref, sem):
    del xs_in_ref
    tm = h_ref.shape[0]

    def start(r, _):
        for k in range(TOP_K):
            _row_copy(h_ref, r, xs_ref, dest_ref[TOP_K * r + k], sem).start()
        return 0

    def wait(r, _):
        for k in range(TOP_K):
            _row_copy(h_ref, r, xs_ref, dest_ref[TOP_K * r + k], sem).wait()
        return 0

    lax.fori_loop(0, tm, start, 0)
    lax.fori_loop(0, tm, wait, 0)


def _dispatch(h, dest, xs):
    t, d = h.shape
    tm = min(ROW_DMA_TILE, t)
    return pl.pallas_call(
        _dispatch_kernel,
        out_shape=jax.ShapeDtypeStruct(xs.shape, xs.dtype),
        grid=(t // tm,),
        in_specs=[pl.BlockSpec((TOP_K * tm,), lambda i: (i,), memory_space=pltpu.SMEM),
                  pl.BlockSpec((tm, d), lambda i: (i, 0)),
                  pl.BlockSpec(memory_space=pl.ANY)],
        out_specs=pl.BlockSpec(memory_space=pl.ANY),
        scratch_shapes=[pltpu.SemaphoreType.DMA],
        input_output_aliases={2: 0},
        compiler_params=_params(4 * tm * d * 4, 1),
        name="moe_dispatch",
    )(dest, h, xs)


def _experts_kernel(te_ref, nu_ref, xs_ref, wg_ref, wu_ref, wd_ref, ys_ref):
    i = pl.program_id(0)

    @pl.when(i < nu_ref[0])
    def _():
        x = xs_ref[...].astype(BF16)
        hg = _dot(x, wg_ref[...].astype(BF16))
        hu = _dot(x, wu_ref[...].astype(BF16))
        hid = (_silu(hg) * hu).astype(BF16)
        ys_ref[...] = _dot(hid, wd_ref[...].astype(BF16))

    @pl.when(i >= nu_ref[0])
    def _():
        ys_ref[...] = jnp.zeros_like(ys_ref)


def _experts(xs, tile_expert, n_used, w_gate, w_up, w_down, layer):
    p, d = xs.shape
    tm = MOE_TILE
    f = w_gate.shape[-1]
    row_map = lambda i, te, nu: (jnp.minimum(i, jnp.maximum(nu[0] - 1, 0)), 0)
    return pl.pallas_call(
        _experts_kernel,
        out_shape=jax.ShapeDtypeStruct((p, d), F32),
        grid_spec=pltpu.PrefetchScalarGridSpec(
            num_scalar_prefetch=2,
            grid=(p // tm,),
            in_specs=[pl.BlockSpec((tm, d), row_map),
                      pl.BlockSpec((None, None, d, f), lambda i, te, nu: (layer, te[i], 0, 0)),
                      pl.BlockSpec((None, None, d, f), lambda i, te, nu: (layer, te[i], 0, 0)),
                      pl.BlockSpec((None, None, f, d), lambda i, te, nu: (layer, te[i], 0, 0))],
            out_specs=pl.BlockSpec((tm, d), lambda i, te, nu: (i, 0))),
        compiler_params=_params(2 * 3 * d * f * 4 + 4 * tm * d * 4 + 3 * d * f * 2 + 8 * tm * f * 4, 1),
        name="moe_experts",
    )(tile_expert, n_used, xs, w_gate, w_up, w_down)


def _combine_kernel(*refs, final):
    if final:
        dest_ref, route_ref, x_ref, gate_ref, fg_ref, ys_ref, o_ref, y_ref, gbuf_ref, sem = refs
    else:
        dest_ref, route_ref, x_ref, gate_ref, ys_ref, o_ref, gbuf_ref, sem = refs
    tm = x_ref.shape[0]

    def start(r, _):
        for k in range(TOP_K):
            _row_copy(ys_ref, dest_ref[TOP_K * r + k], gbuf_ref.at[k], r, sem).start()
        return 0

    def wait(r, _):
        for k in range(TOP_K):
            _row_copy(ys_ref, dest_ref[TOP_K * r + k], gbuf_ref.at[k], r, sem).wait()
        return 0

    lax.fori_loop(0, tm, start, 0)
    lax.fori_loop(0, tm, wait, 0)
    route = route_ref[...]
    moe = gbuf_ref[0] * route[:, ROUTE_LANE_W:ROUTE_LANE_W + 1]
    for k in range(1, TOP_K):
        moe = moe + gbuf_ref[k] * route[:, ROUTE_LANE_W + k:ROUTE_LANE_W + k + 1]
    x = x_ref[...] + _row(gate_ref) * moe
    o_ref[...] = x
    if final:
        y_ref[...] = _rmsnorm(x, fg_ref[...])


def _combine(dest, route, x_mid, mod, ys, final_g):
    t, d = x_mid.shape
    tm = min(ROW_DMA_TILE, t)
    final = final_g is not None
    row = lambda w: pl.BlockSpec((tm, w), lambda i: (i, 0))
    in_specs = [pl.BlockSpec((TOP_K * tm,), lambda i: (i,), memory_space=pltpu.SMEM),
                row(LANES), row(d), mod.spec(5, tm)]
    args = [dest, route, x_mid, mod.arr]
    if final:
        in_specs.append(pl.BlockSpec((1, d), lambda i: (0, 0)))
        args.append(final_g.reshape(1, d))
    in_specs.append(pl.BlockSpec(memory_space=pl.ANY))
    args.append(ys)
    out_shape = [jax.ShapeDtypeStruct((t, d), F32)] * (2 if final else 1)
    out_specs = [row(d)] * (2 if final else 1)
    return pl.pallas_call(
        functools.partial(_combine_kernel, final=final),
        out_shape=out_shape,
        grid=(t // tm,),
        in_specs=in_specs,
        out_specs=out_specs,
        scratch_shapes=[pltpu.VMEM((TOP_K, tm, d), F32), pltpu.SemaphoreType.DMA],
        compiler_params=_params(8 * tm * d * 4 + TOP_K * tm * d * 4, 1),
        name="moe_combine",
    )(*args)


def _moe(layer, h_p, h_s, logit_p, logit_s, xmid_p, xmid_s, mod_p, mod_s, w_gate, w_up, w_down, final_g):
    tp, d = h_p.shape
    ts = h_s.shape[0]
    route, counts = _route(jnp.concatenate([logit_p, logit_s], axis=0))
    cnt = counts[0, ROUTER_LANE0:ROUTER_LANE0 + N_EXPERTS].astype(I32)
    padded = (cnt + MOE_TILE - 1) // MOE_TILE * MOE_TILE
    ends = jnp.cumsum(padded)
    starts = ends - padded
    expert = route[:, ROUTE_LANE_E:ROUTE_LANE_E + TOP_K].astype(I32)
    rank = route[:, ROUTE_LANE_R:ROUTE_LANE_R + TOP_K].astype(I32)
    dest = (starts[expert] + rank).reshape(-1)
    n_rows = (tp + ts) * TOP_K + N_EXPERTS * MOE_TILE
    n_tiles = n_rows // MOE_TILE
    n_used = (ends[-1:] // MOE_TILE).astype(I32)
    tile_expert = jnp.minimum(
        jnp.searchsorted(ends, jnp.arange(n_tiles, dtype=I32) * MOE_TILE, side="right"), N_EXPERTS - 1).astype(I32)
    xs = jnp.zeros((n_rows, d), F32)
    xs = _dispatch(h_p, dest[:tp * TOP_K], xs)
    xs = _dispatch(h_s, dest[tp * TOP_K:], xs)
    ys = _experts(xs, tile_expert, n_used, w_gate, w_up, w_down, layer)
    out_p = _combine(dest[:tp * TOP_K], route[:tp], xmid_p, mod_p, ys, final_g)
    out_s = _combine(dest[tp * TOP_K:], route[tp:], xmid_s, mod_s, ys, final_g)
    return out_p, out_s


def kernel(x_prompt, x_sample, c_prompt, c_sample, state_ssm, state_ssd_conv, cache_k, cache_v, page_table, state_short_conv, ada_w, ada_b, ln_mix_g, ln_ffn_g, mix_w_in, mix_w_out, ssd_conv_w, ssd_conv_b, ssd_dt_bias, ssd_a_log, ssd_d, ssd_norm_g, sb_bias, sc_w_in, sc_conv_w, sc_w_out, moe_w_group, moe_b_group, moe_w_router, moe_b_router, moe_w_gate, moe_w_up, moe_w_down, final_g):
    b, s, d = x_prompt.shape
    bs, ls, _ = x_sample.shape
    tp, ts = b * s, bs * ls
    depth = ada_w.shape[0]
    xp = x_prompt.reshape(tp, d)
    xs = x_sample.reshape(ts, d)
    mod = _adaln(jnp.concatenate([c_prompt, c_sample], axis=0), ada_w, ada_b)
    ls_pad = SUBLANES

    def pad_rows(a, rows, front=0):
        return jnp.pad(a, ((0, 0), (front, rows - a.shape[1] - front), (0, 0)))

    ssm_p, ssm_s, cv_p, cv_s, k_p, k_s, v_p, v_s, sc_p, sc_s = ([] for _ in range(10))
    y_out = None
    for layer in range(depth):
        i = layer // 2
        mod_p = _Mod(mod[layer, :b].reshape(b, 1, 6 * d), s, d)
        mod_s = _Mod(jnp.repeat(mod[layer, b:], ls, axis=0), ls, d)
        n_route = N_EXPERT_GROUPS + N_EXPERTS
        w_route = jnp.pad(jnp.concatenate([moe_w_group[layer], moe_w_router[layer]], axis=1),
                          ((0, 0), (0, LANES - n_route)))
        b_route = jnp.pad(jnp.concatenate([moe_b_group[layer], moe_b_router[layer]]),
                          (0, LANES - n_route)).reshape(1, LANES)
        if layer % 2 == 0:
            w = mix_w_in[i]
            o_dt = D_SSM + CONV_DIM
            o_q = o_dt + SSM_HEADS
            w_bf = jnp.concatenate(
                [w[:, :o_dt], w[:, o_q:o_q + D_ATT] * SB_SCALE, w[:, o_q + D_ATT:], w[:, o_dt:o_q],
                 jnp.zeros((d, LANES - SSM_HEADS), F32)], axis=1).astype(BF16)
            c0 = D_SSM + CONV_DIM
            splits = ((0, D_SSM, (F32,)), (D_SSM, CONV_DIM, (F32,)), (c0, D_ATT, (BF16,)),
                      (c0 + D_ATT, D_ATT, (F32, BF16)), (c0 + 2 * D_ATT, D_ATT, (F32, BF16)),
                      (c0 + 3 * D_ATT, LANES, (F32,)))
            z1, xbc1, q1, kf1, kb1, vf1, vb1, dt1 = _pre_mix(xp, ln_mix_g[layer], mod_p, w_bf, splits, "pre_mix_even_p")
            z2, xbc2, q2, kf2, kb2, vf2, vb2, dt2 = _pre_mix(xs, ln_mix_g[layer], mod_s, w_bf, splits, "pre_mix_even_s")
            prm = (ssd_conv_w[i], ssd_conv_b[i].reshape(1, CONV_DIM),
                   jnp.pad(ssd_dt_bias[i], (0, LANES - SSM_HEADS)).reshape(1, LANES),
                   jnp.pad(ssd_a_log[i], (0, LANES - SSM_HEADS)).reshape(1, LANES),
                   jnp.repeat(ssd_d[i], SSM_HEAD_DIM).reshape(1, D_SSM),
                   ssd_norm_g[i].reshape(1, D_SSM))
            y1, hfin1, cv1 = _ssd(xbc1, z1, dt1, prm, b, SSD_CHUNK, SSD_CHUNK, name="ssd_prompt")
            seq_pad = lambda a: pad_rows(a.reshape(bs, ls, a.shape[-1]), ls_pad).reshape(bs * ls_pad, a.shape[-1])
            y2, hfin2, cv2 = _ssd(
                seq_pad(xbc2), seq_pad(z2), seq_pad(dt2), prm, bs, ls_pad, ls,
                tail0=pad_rows(state_ssd_conv[i], SUBLANES, front=SUBLANES - (SSD_CONV - 1)),
                h0=state_ssm[i].reshape(bs, D_SSM, D_STATE), name="ssd_sample")
            y2 = y2.reshape(bs, ls_pad, D_SSM)[:, :ls].reshape(ts, D_SSM)
            att1 = _attn_prompt(q1, kb1, vb1, sb_bias[i], b)
            n_pool, page = cache_k.shape[1], cache_k.shape[2]
            att2 = _attn_sample(
                q2.reshape(bs, ls, D_ATT), pad_rows(kf2.reshape(bs, ls, D_ATT), SUBLANES),
                pad_rows(vf2.reshape(bs, ls, D_ATT), SUBLANES),
                cache_k[i].reshape(n_pool, page, D_ATT), cache_v[i].reshape(n_pool, page, D_ATT),
                page_table, sb_bias[i]).reshape(ts, D_ATT)
            w_out = mix_w_out[i].astype(BF16)
            w_list = [w_out[:D_SSM], w_out[D_SSM:]]
            xm1, hf1, lg1 = _post_mix([y1, att1], w_list, xp, ln_ffn_g[layer], mod_p, w_route, b_route, "post_mix_even_p")
            xm2, hf2, lg2 = _post_mix([y2, att2], w_list, xs, ln_ffn_g[layer], mod_s, w_route, b_route, "post_mix_even_s")
            ssm_p.append(hfin1.reshape(b, SSM_HEADS, SSM_HEAD_DIM, D_STATE))
            ssm_s.append(hfin2.reshape(bs, SSM_HEADS, SSM_HEAD_DIM, D_STATE))
            cv_p.append(cv1)
            cv_s.append(cv2)
            k_p.append(kf1.reshape(b, s, ATT_HEADS, ATT_HEAD_DIM))
            k_s.append(kf2.reshape(bs, ls, ATT_HEADS, ATT_HEAD_DIM))
            v_p.append(vf1.reshape(b, s, ATT_HEADS, ATT_HEAD_DIM))
            v_s.append(vf2.reshape(bs, ls, ATT_HEADS, ATT_HEAD_DIM))
        else:
            w_bf = sc_w_in[i].astype(BF16)
            splits = tuple((j * d, d, (F32,)) for j in range(3))
            bg1, cg1, u1 = _pre_mix(xp, ln_mix_g[layer], mod_p, w_bf, splits, "pre_mix_odd_p")
            bg2, cg2, u2 = _pre_mix(xs, ln_mix_g[layer], mod_s, w_bf, splits, "pre_mix_odd_s")
            a1, st1 = _sc_prompt(bg1, cg1, u1, sc_conv_w[i], b)
            a2, st2 = _sc_sample(bg2, cg2, u2, state_short_conv[i], sc_conv_w[i], bs)
            w_list = [sc_w_out[i].astype(BF16)]
            xm1, hf1, lg1 = _post_mix([a1], w_list, xp, ln_ffn_g[layer], mod_p, w_route, b_route, "post_mix_odd_p")
            xm2, hf2, lg2 = _post_mix([a2], w_list, xs, ln_ffn_g[layer], mod_s, w_route, b_route, "post_mix_odd_s")
            sc_p.append(st1)
            sc_s.append(st2)
        fg = final_g if layer == depth - 1 else None
        out_p, out_s = _moe(layer, hf1, hf2, lg1, lg2, xm1, xm2, mod_p, mod_s,
                            moe_w_gate, moe_w_up, moe_w_down, fg)
        xp, xs = out_p[0], out_s[0]
        if fg is not None:
            y_out = (out_p[1].reshape(b, s, d), out_s[1].reshape(bs, ls, d))
    return (y_out[0], y_out[1], jnp.stack(ssm_p), jnp.stack(ssm_s), jnp.stack(cv_p), jnp.stack(cv_s),
            jnp.stack(k_p), jnp.stack(k_s), jnp.stack(v_p), jnp.stack(v_s), jnp.stack(sc_p), jnp.stack(sc_s))
```

```python
import functools

import jax
import jax.numpy as jnp
from jax import lax
from jax.experimental import pallas as pl
from jax.experimental.pallas import tpu as pltpu

F32 = jnp.float32
BF16 = jnp.bfloat16
I32 = jnp.int32

SSM_HEADS = 16
SSM_HEAD_DIM = 64
D_SSM = SSM_HEADS * SSM_HEAD_DIM
SSM_GROUPS = 2
HEADS_PER_GROUP = SSM_HEADS // SSM_GROUPS
D_STATE = 128
SSD_CONV = 4
SSD_CHUNK = 128
CONV_DIM = D_SSM + 2 * SSM_GROUPS * D_STATE
ATT_HEADS = 16
ATT_HEAD_DIM = 64
D_ATT = ATT_HEADS * ATT_HEAD_DIM
SB_SCALE = ATT_HEAD_DIM ** -0.5
SC_WIDTH = 3
N_EXPERT_GROUPS = 4
EXPERTS_PER_GROUP = 8
N_EXPERTS = N_EXPERT_GROUPS * EXPERTS_PER_GROUP
TOP_K = 2
D_EXPERT = 256
RMS_EPS = 1e-6

LANES = 128
SUBLANES = 8
VMEM_BYTES_V7X = 64 * 1024 * 1024
VMEM_LIMIT_CAP = VMEM_BYTES_V7X - 8 * 1024 * 1024

TOKEN_TILE = 256
ATT_TILE = 256
ATT_HEAD_BLOCK = 4
SAMPLE_PAGES_PER_STEP = 8
MOE_TILE = 256
ROW_DMA_TILE = 512
ROW_DMA_UNROLL = 8
ROUTE_LANE_E = 0
ROUTE_LANE_R = 2
ROUTE_LANE_W = 4
ROUTER_LANE0 = N_EXPERT_GROUPS


def _vmem_limit(nbytes):
    return int(min(max(2 * nbytes, 32 * 1024 * 1024), VMEM_LIMIT_CAP))


def _params(nbytes, n_axes):
    return pltpu.CompilerParams(dimension_semantics=("arbitrary",) * n_axes,
                                vmem_limit_bytes=_vmem_limit(nbytes))


def _dot(a, b):
    return jnp.dot(a, b, preferred_element_type=F32)


def _dot_nt(a, b):
    return lax.dot_general(a, b, (((1,), (1,)), ((), ())), preferred_element_type=F32)


def _dot_tn(a, b):
    return lax.dot_general(a, b, (((0,), (0,)), ((), ())), preferred_element_type=F32)


def _split2(x):
    hi = x.astype(BF16)
    lo = (x - hi.astype(F32)).astype(BF16)
    return hi, lo


def _split3(x):
    hi = x.astype(BF16)
    r = x - hi.astype(F32)
    mid = r.astype(BF16)
    lo = (r - mid.astype(F32)).astype(BF16)
    return hi, mid, lo


def _dot3(a, b):
    ah, al = _split2(a)
    bh, bl = _split2(b)
    return _dot(ah, bh) + (_dot(al, bh) + _dot(ah, bl))


def _silu(x):
    return x / (1.0 + jnp.exp(-x))


def _softplus(x):
    return jnp.maximum(x, 0.0) + jnp.log(1.0 + jnp.exp(-jnp.abs(x)))


def _rmsnorm(x, g):
    return x * lax.rsqrt(jnp.mean(x * x, axis=-1, keepdims=True) + RMS_EPS) * g


def _row(ref):
    v = ref[...]
    return v[0] if v.ndim == 3 else v


def _adaln_kernel(c_ref, w_ref, b_ref, o_ref):
    o_ref[...] = _dot3(_silu(c_ref[...]), w_ref[...]) + b_ref[...]


def _adaln(c_all, ada_w, ada_b):
    depth, d, n = ada_w.shape
    rows = c_all.shape[0]
    tn = 1024
    return pl.pallas_call(
        _adaln_kernel,
        out_shape=jax.ShapeDtypeStruct((depth, rows, n), F32),
        grid=(depth, n // tn),
        in_specs=[pl.BlockSpec((rows, d), lambda l, j: (0, 0)),
                  pl.BlockSpec((None, d, tn), lambda l, j: (l, 0, j)),
                  pl.BlockSpec((None, 1, tn), lambda l, j: (l, 0, j))],
        out_specs=pl.BlockSpec((None, rows, tn), lambda l, j: (l, 0, j)),
        compiler_params=_params(2 * d * tn * 4 + 4 * rows * (d + tn) * 4, 2),
        name="adaln",
    )(c_all, ada_w, ada_b.reshape(depth, 1, n))


class _Mod:
    def __init__(self, arr, seq_len, d):
        self.arr = arr
        self.seq_len = seq_len
        self.d = d

    def spec(self, which, tm):
        d = self.d
        if self.arr.ndim == 3:
            tiles_per_seq = self.seq_len // tm
            return pl.BlockSpec((1, 1, d), lambda i, *_: (i // tiles_per_seq, 0, which))
        return pl.BlockSpec((tm, d), lambda i, *_: (i, which))


def _pre_mix_kernel(x_ref, g_ref, shift_ref, scale_ref, w_ref, *out_refs, splits):
    h = (_rmsnorm(x_ref[...], g_ref[...]) * (1.0 + _row(scale_ref)) + _row(shift_ref)).astype(BF16)
    k = 0
    for start, width, dtypes in splits:
        acc = _dot(h, w_ref[:, start:start + width])
        for dt in dtypes:
            out_refs[k][...] = acc.astype(dt)
            k += 1


def _pre_mix(x, g, mod, w_bf, splits, name):
    t, d = x.shape
    n = w_bf.shape[1]
    tm = TOKEN_TILE
    out_shape, out_specs, out_bytes = [], [], 0
    for _, width, dtypes in splits:
        for dt in dtypes:
            out_shape.append(jax.ShapeDtypeStruct((t, width), dt))
            out_specs.append(pl.BlockSpec((tm, width), lambda i: (i, 0)))
            out_bytes += tm * width * jnp.dtype(dt).itemsize
    return pl.pallas_call(
        functools.partial(_pre_mix_kernel, splits=splits),
        out_shape=out_shape,
        grid=(t // tm,),
        in_specs=[pl.BlockSpec((tm, d), lambda i: (i, 0)),
                  pl.BlockSpec((1, d), lambda i: (0, 0)),
                  mod.spec(0, tm), mod.spec(1, tm),
                  pl.BlockSpec((d, n), lambda i: (0, 0))],
        out_specs=out_specs,
        compiler_params=_params(2 * d * n * 2 + 2 * tm * d * 4 + 2 * out_bytes + tm * n * 4, 1),
        name=name,
    )(x, g.reshape(1, d), mod.arr, mod.arr, w_bf)


def _ssd_kernel(*refs, chunk, n_real, has_init):
    if has_init:
        (xbc_ref, z_ref, dt_ref, tail0_ref, h0_ref, cw_ref, cb_ref, dtb_ref, alog_ref, dsk_ref, ng_ref,
         y_ref, hout_ref, cvout_ref, buf_ref, state_ref, ybuf_ref) = refs
    else:
        (xbc_ref, z_ref, dt_ref, cw_ref, cb_ref, dtb_ref, alog_ref, dsk_ref, ng_ref,
         y_ref, hout_ref, cvout_ref, buf_ref, state_ref, ybuf_ref) = refs
    c = pl.program_id(1)
    last = pl.num_programs(1) - 1
    L = chunk
    T0 = SUBLANES

    @pl.when(c == 0)
    def _():
        if has_init:
            buf_ref[0:T0, :] = tail0_ref[0]
            state_ref[...] = h0_ref[0]
        else:
            buf_ref[0:T0, :] = jnp.zeros((T0, CONV_DIM), F32)
            state_ref[...] = jnp.zeros_like(state_ref)

    xbc = xbc_ref[...]
    buf_ref[T0:T0 + L, :] = xbc
    conv = xbc * cw_ref[3:4, :]
    for j in range(SSD_CONV - 1):
        conv = conv + buf_ref[pl.ds(T0 - (SSD_CONV - 1) + j, L), :] * cw_ref[j:j + 1, :]
    xc = _silu(conv + cb_ref[...])
    xs = xc[:, :D_SSM]
    bmat = xc[:, D_SSM:D_SSM + SSM_GROUPS * D_STATE].astype(BF16)
    cmat = xc[:, D_SSM + SSM_GROUPS * D_STATE:].astype(BF16)

    @pl.when(c == last)
    def _():
        cvout_ref[0] = buf_ref[pl.ds(T0 + n_real - (SSD_CONV - 1), SSD_CONV - 1), :]

    if n_real == L:
        buf_ref[0:T0, :] = buf_ref[L:L + T0, :]

    dt = _softplus(dt_ref[...] + dtb_ref[...])
    rows = lax.broadcasted_iota(I32, (L, L), 0)
    cols = lax.broadcasted_iota(I32, (L, L), 1)
    if n_real < L:
        dt = jnp.where(lax.broadcasted_iota(I32, dt.shape, 0) < n_real, dt, 0.0)
    a = -jnp.exp(alog_ref[...])
    tri = cols <= rows
    tri_bf = jnp.where(tri, 1.0, 0.0).astype(BF16)
    da_hi, da_mid, da_lo = _split3(dt * a)
    acum = _dot(tri_bf, da_hi) + (_dot(tri_bf, da_mid) + _dot(tri_bf, da_lo))
    acum_t = acum.T
    a_last = acum[L - 1:L, :]
    to_end = jnp.exp(a_last - acum)
    e_acum = jnp.exp(acum)
    chunk_decay = jnp.exp(a_last)

    cb = [_dot_nt(cmat[:, g * D_STATE:(g + 1) * D_STATE], bmat[:, g * D_STATE:(g + 1) * D_STATE])
          for g in range(SSM_GROUPS)]
    for h in range(SSM_HEADS):
        g = h // HEADS_PER_GROUP
        hs = slice(h * SSM_HEAD_DIM, (h + 1) * SSM_HEAD_DIM)
        gs = slice(g * D_STATE, (g + 1) * D_STATE)
        seg = acum[:, h:h + 1] - acum_t[h:h + 1, :]
        decay = jnp.exp(jnp.where(tri, seg, -jnp.inf))
        m = (cb[g] * decay).astype(BF16)
        xh = xs[:, hs]
        xdt = xh * dt[:, h:h + 1]
        y_diag = _dot(m, xdt.astype(BF16))
        st = state_ref[hs, :]
        y_off = _dot_nt(cmat[:, gs], st.astype(BF16)) * e_acum[:, h:h + 1]
        s_new = _dot_tn((xdt * to_end[:, h:h + 1]).astype(BF16), bmat[:, gs])
        state_ref[hs, :] = st * chunk_decay[:, h:h + 1] + s_new
        ybuf_ref[:, hs] = y_diag + y_off + xh * dsk_ref[:, hs]

    y = ybuf_ref[...] * _silu(z_ref[...])
    half = D_SSM // SSM_GROUPS
    for g in range(SSM_GROUPS):
        yg = y[:, g * half:(g + 1) * half]
        y_ref[:, g * half:(g + 1) * half] = (
            yg * lax.rsqrt(jnp.mean(yg * yg, axis=-1, keepdims=True) + RMS_EPS) * ng_ref[:, g * half:(g + 1) * half]
        ).astype(y_ref.dtype)

    @pl.when(c == last)
    def _():
        hout_ref[0] = state_ref[...]


def _ssd(xbc, z, dt, prm, n_seq, chunk, n_real, tail0=None, h0=None, name="ssd"):
    t = xbc.shape[0]
    nc = t // (n_seq * chunk)
    has_init = tail0 is not None
    cw, cb, dtb, alog, dsk, ng = prm
    row_spec = lambda w: pl.BlockSpec((chunk, w), lambda b, c: (b * nc + c, 0))
    const = lambda shp: pl.BlockSpec(shp, lambda b, c: (0,) * len(shp))
    in_specs = [row_spec(CONV_DIM), row_spec(D_SSM), row_spec(LANES)]
    args = [xbc, z, dt]
    if has_init:
        in_specs += [pl.BlockSpec((1, SUBLANES, CONV_DIM), lambda b, c: (b, 0, 0)),
                     pl.BlockSpec((1, D_SSM, D_STATE), lambda b, c: (b, 0, 0))]
        args += [tail0, h0]
    in_specs += [const((SSD_CONV, CONV_DIM)), const((1, CONV_DIM)), const((1, LANES)), const((1, LANES)),
                 const((1, D_SSM)), const((1, D_SSM))]
    args += [cw, cb, dtb, alog, dsk, ng]
    scratch_bytes = ((chunk + 2 * SUBLANES) * CONV_DIM + D_SSM * D_STATE + chunk * D_SSM) * 4
    block_bytes = chunk * (CONV_DIM + D_SSM + LANES) * 4 + chunk * D_SSM * 2 + 2 * D_SSM * D_STATE * 4
    return pl.pallas_call(
        functools.partial(_ssd_kernel, chunk=chunk, n_real=n_real, has_init=has_init),
        out_shape=[jax.ShapeDtypeStruct((t, D_SSM), BF16),
                   jax.ShapeDtypeStruct((n_seq, D_SSM, D_STATE), F32),
                   jax.ShapeDtypeStruct((n_seq, SSD_CONV - 1, CONV_DIM), F32)],
        grid=(n_seq, nc),
        in_specs=in_specs,
        out_specs=[row_spec(D_SSM),
                   pl.BlockSpec((1, D_SSM, D_STATE), lambda b, c: (b, 0, 0)),
                   pl.BlockSpec((1, SSD_CONV - 1, CONV_DIM), lambda b, c: (b, 0, 0))],
        scratch_shapes=[pltpu.VMEM((chunk + 2 * SUBLANES, CONV_DIM), F32),
                        pltpu.VMEM((D_SSM, D_STATE), F32),
                        pltpu.VMEM((chunk, D_SSM), F32)],
        compiler_params=_params(2 * block_bytes + scratch_bytes + 8 * chunk * CONV_DIM * 4, 2),
        name=name,
    )(*args)


MASKED_LOGIT = -1e30


def _sb_scores(s, neg_upper2):
    sp = _softplus(s)
    hi, lo = _split2(sp)
    later = _dot(jnp.concatenate([hi, lo], axis=1), neg_upper2)
    return s - sp, later, later[:, 0:1] - sp[:, 0:1]


def _neg_strict_upper2(n):
    j = lax.broadcasted_iota(I32, (2 * n, n), 0) % n
    k = lax.broadcasted_iota(I32, (2 * n, n), 1)
    return jnp.where(j > k, -1.0, 0.0).astype(BF16)


def _attn_prompt_kernel(bias_ref, q_ref, k_ref, v_ref, o_ref, qs_ref, carry_ref, acc_ref):
    hq = pl.program_id(1)
    qi = pl.program_id(2)
    tq, width = q_ref.shape
    tk = ATT_TILE
    nh = ATT_HEAD_BLOCK
    m = nh * tq
    lane_head = lax.broadcasted_iota(I32, (1, width), 1) // ATT_HEAD_DIM
    q = q_ref[...]
    for h in range(nh):
        qs_ref[h * tq:(h + 1) * tq, :] = jnp.where(lane_head == h, q, jnp.zeros_like(q))
    row_head = lax.broadcasted_iota(I32, (m, 1), 0) // tq
    bias = jnp.zeros((m, 1), F32)
    for h in range(nh):
        bias = jnp.where(row_head == h, bias_ref[hq * nh + h], bias)
    neg_upper2 = _neg_strict_upper2(tk)
    causal = lax.broadcasted_iota(I32, (m, tk), 1) < lax.broadcasted_iota(I32, (m, tk), 0) % tq

    def tile(j, mask):
        start = pl.multiple_of(j * tk, tk)
        vt = v_ref[pl.ds(start, tk), :]
        s = _dot_nt(qs_ref[...], k_ref[pl.ds(start, tk), :]) + bias
        if mask is not None:
            s = jnp.where(mask, s, MASKED_LOGIT)
        log_beta, later, total = _sb_scores(s, neg_upper2)
        carry = carry_ref[...]
        w = jnp.exp(log_beta + (later + carry)).astype(BF16)
        carry_ref[...] = carry + total
        w_cat = jnp.concatenate([w[h * tq:(h + 1) * tq] for h in range(nh)], axis=1)
        v_cat = jnp.concatenate([jnp.where(lane_head == h, vt, jnp.zeros_like(vt)) for h in range(nh)], axis=0)
        acc_ref[...] += _dot(w_cat, v_cat)

    carry_ref[...] = jnp.zeros_like(carry_ref)
    acc_ref[...] = jnp.zeros_like(acc_ref)
    tile(qi, causal)

    def body(i, c):
        tile(qi - 1 - i, None)
        return c

    lax.fori_loop(0, qi, body, 0)
    o_ref[...] = acc_ref[...].astype(o_ref.dtype)


def _attn_prompt(q, k, v, bias, n_seq):
    t, d = q.shape
    s = t // n_seq
    tq = ATT_TILE
    nq = s // tq
    width = ATT_HEAD_BLOCK * ATT_HEAD_DIM
    return pl.pallas_call(
        _attn_prompt_kernel,
        out_shape=jax.ShapeDtypeStruct((t, d), BF16),
        grid_spec=pltpu.PrefetchScalarGridSpec(
            num_scalar_prefetch=1,
            grid=(n_seq, d // width, nq),
            in_specs=[pl.BlockSpec((tq, width), lambda b, hq, qi, *_: (b * nq + qi, hq)),
                      pl.BlockSpec((s, width), lambda b, hq, qi, *_: (b, hq)),
                      pl.BlockSpec((s, width), lambda b, hq, qi, *_: (b, hq))],
            out_specs=pl.BlockSpec((tq, width), lambda b, hq, qi, *_: (b * nq + qi, hq)),
            scratch_shapes=[pltpu.VMEM((ATT_HEAD_BLOCK * tq, width), BF16),
                            pltpu.VMEM((ATT_HEAD_BLOCK * tq, 1), F32),
                            pltpu.VMEM((tq, width), F32)]),
        compiler_params=_params(4 * s * width * 2 + 4 * tq * width * 2
                                + 12 * ATT_HEAD_BLOCK * tq * ATT_TILE * 4, 3),
        name="sb_attn_prompt",
    )(bias, q, k, v)


def _attn_sample_kernel(pt_ref, bias_ref, q_ref, kn_ref, vn_ref, *refs, n_group):
    k_refs = refs[:n_group]
    v_refs = refs[n_group:2 * n_group]
    o_ref, carry_ref, acc_ref = refs[2 * n_group:]
    g = pl.program_id(1)
    n_rows = q_ref.shape[1]
    blk = ATT_TILE
    q = q_ref[0]
    row_head = lax.broadcasted_iota(I32, (n_rows, 1), 0) % ATT_HEADS
    bias = jnp.zeros((n_rows, 1), F32)
    for h in range(ATT_HEADS):
        bias = jnp.where(row_head == h, bias_ref[h], bias)
    neg_upper2 = _neg_strict_upper2(blk)

    def attend(k2, v2, mask_fn):
        nb = k2.shape[0] // blk
        m = nb * n_rows
        s = _dot_nt(q, k2)
        s = jnp.concatenate([s[:, j * blk:(j + 1) * blk] for j in range(nb)], axis=0)
        s = s + jnp.concatenate([bias] * nb, axis=0)
        row = lax.broadcasted_iota(I32, (m, blk), 0)
        col = lax.broadcasted_iota(I32, (m, blk), 1)
        s = jnp.where(mask_fn(row % n_rows, col), s, MASKED_LOGIT)
        log_beta, later, total = _sb_scores(s, neg_upper2)
        carry = carry_ref[...]
        carries = [None] * nb
        for j in range(nb - 1, -1, -1):
            carries[j] = carry
            carry = carry + total[j * n_rows:(j + 1) * n_rows]
        carry_ref[...] = carry
        w = jnp.exp(log_beta + (later + jnp.concatenate(carries, axis=0))).astype(BF16)
        w = jnp.concatenate([w[j * n_rows:(j + 1) * n_rows] for j in range(nb)], axis=1)
        acc_ref[...] += _dot(w, v2)

    own_head = lambda r, c: c % ATT_HEADS == r % ATT_HEADS

    @pl.when(g == 0)
    def _():
        carry_ref[...] = jnp.zeros_like(carry_ref)
        acc_ref[...] = jnp.zeros_like(acc_ref)
        pad = jnp.zeros((blk - kn_ref.shape[1], kn_ref.shape[2]), F32)
        k2 = jnp.concatenate([kn_ref[0], pad], axis=0).astype(BF16)
        v2 = jnp.concatenate([vn_ref[0], pad], axis=0).astype(BF16)
        attend(k2, v2, lambda r, c: own_head(r, c) & (c // ATT_HEADS < r // ATT_HEADS))

    for i in range(n_group):
        k3 = k_refs[i][...]
        v3 = v_refs[i][...]
        n_cols = k3.shape[0] * k3.shape[1]
        attend(k3.reshape(n_cols, k3.shape[2]).astype(BF16), v3.reshape(n_cols, v3.shape[2]).astype(BF16), own_head)

    @pl.when(g == pl.num_programs(1) - 1)
    def _():
        o_ref[0] = acc_ref[...].astype(o_ref.dtype)


def _attn_sample(q, k_new, v_new, cache_k, cache_v, layer, page_table, bias):
    bs, n_rows, dh = q.shape
    n_pages = page_table.shape[1]
    page, n_heads = cache_k.shape[2], cache_k.shape[3]
    n_group = min(SAMPLE_PAGES_PER_STEP, n_pages)
    assert n_pages % n_group == 0
    n_steps = n_pages // n_group

    def page_spec(i):
        return pl.BlockSpec((None, None, page, n_heads, dh),
                            lambda b, g, pt, bias_: (layer, pt[b, n_pages - 1 - (g * n_group + i)], 0, 0, 0))

    seq_spec = pl.BlockSpec((1, n_rows, dh), lambda b, g, pt, bias_: (b, 0, 0))
    page_bytes = page * n_heads * LANES * 4
    return pl.pallas_call(
        functools.partial(_attn_sample_kernel, n_group=n_group),
        out_shape=jax.ShapeDtypeStruct((bs, n_rows, dh), BF16),
        grid_spec=pltpu.PrefetchScalarGridSpec(
            num_scalar_prefetch=2,
            grid=(bs, n_steps),
            in_specs=[seq_spec, seq_spec, seq_spec] + [page_spec(i) for i in range(n_group)] * 2,
            out_specs=seq_spec,
            scratch_shapes=[pltpu.VMEM((n_rows, 1), F32), pltpu.VMEM((n_rows, dh), F32)]),
        compiler_params=_params(2 * 2 * n_group * page_bytes + 8 * page_bytes, 2),
        name="sb_attn_sample",
    )(page_table, bias, q, k_new, v_new, *([cache_k] * n_group), *([cache_v] * n_group))


def _sc_prompt_kernel(b_ref, c_ref, u_ref, w_ref, a_ref, st_ref, buf_ref):
    L = c_ref.shape[0]
    T0 = SUBLANES
    cu = c_ref[...] * u_ref[...]
    buf_ref[0:T0, :] = jnp.zeros((T0, cu.shape[1]), F32)
    buf_ref[T0:T0 + L, :] = cu
    conv = cu * w_ref[SC_WIDTH - 1:SC_WIDTH, :]
    for j in range(SC_WIDTH - 1):
        conv = conv + buf_ref[pl.ds(T0 - (SC_WIDTH - 1) + j, L), :] * w_ref[j:j + 1, :]
    a_ref[...] = (b_ref[...] * conv).astype(a_ref.dtype)
    st_ref[0] = buf_ref[pl.ds(T0 + L - (SC_WIDTH - 1), SC_WIDTH - 1), :]


def _sc_prompt(bg, cg, u, conv_w, n_seq):
    t, d = cg.shape
    s = t // n_seq
    wl = 256
    spec = pl.BlockSpec((s, wl), lambda b, j: (b, j))
    return pl.pallas_call(
        _sc_prompt_kernel,
        out_shape=[jax.ShapeDtypeStruct((t, d), BF16),
                   jax.ShapeDtypeStruct((n_seq, SC_WIDTH - 1, d), F32)],
        grid=(n_seq, d // wl),
        in_specs=[spec, spec, spec, pl.BlockSpec((SC_WIDTH, wl), lambda b, j: (0, j))],
        out_specs=[spec, pl.BlockSpec((1, SC_WIDTH - 1, wl), lambda b, j: (b, 0, j))],
        scratch_shapes=[pltpu.VMEM((s + SUBLANES, wl), F32)],
        compiler_params=_params(2 * 3 * s * wl * 4 + 2 * s * wl * 2 + 6 * s * wl * 4, 2),
        name="short_conv_prompt",
    )(bg, cg, u, conv_w)


def _sc_sample_kernel(b_ref, c_ref, u_ref, st_ref, w_ref, a_ref, sto_ref, *, n_new, d):
    up = [st_ref[:, j * d:(j + 1) * d] for j in range(SC_WIDTH - 1)]
    up += [c_ref[:, t * d:(t + 1) * d] * u_ref[:, t * d:(t + 1) * d] for t in range(n_new)]
    for t in range(n_new):
        conv = up[t] * w_ref[0:1, :]
        for j in range(1, SC_WIDTH):
            conv = conv + up[t + j] * w_ref[j:j + 1, :]
        a_ref[:, t * d:(t + 1) * d] = (b_ref[:, t * d:(t + 1) * d] * conv).astype(a_ref.dtype)
    for j in range(SC_WIDTH - 1):
        sto_ref[:, j * d:(j + 1) * d] = up[n_new + j]


def _sc_sample(bg, cg, u, state, conv_w, n_seq):
    t, d = cg.shape
    n_new = t // n_seq
    wide = lambda x: x.reshape(n_seq, n_new * d)
    a, st = pl.pallas_call(
        functools.partial(_sc_sample_kernel, n_new=n_new, d=d),
        out_shape=[jax.ShapeDtypeStruct((n_seq, n_new * d), BF16),
                   jax.ShapeDtypeStruct((n_seq, (SC_WIDTH - 1) * d), F32)],
        compiler_params=pltpu.CompilerParams(vmem_limit_bytes=_vmem_limit(16 * n_seq * n_new * d * 4)),
        name="short_conv_sample",
    )(wide(bg), wide(cg), wide(u), state.reshape(n_seq, (SC_WIDTH - 1) * d), conv_w)
    return a.reshape(t, d), st.reshape(n_seq, SC_WIDTH - 1, d)


def _post_mix_kernel(*refs, n_in):
    a_refs = refs[:n_in]
    w_refs = refs[n_in:2 * n_in]
    x_ref, gate_ref, g_ref, shift_ref, scale_ref, wr_ref, br_ref, xmid_ref, h_ref, logit_ref = refs[2 * n_in:]
    acc = _dot(a_refs[0][...], w_refs[0][...])
    for a_ref, w_ref in zip(a_refs[1:], w_refs[1:]):
        acc = acc + _dot(a_ref[...], w_ref[...])
    xm = x_ref[...] + _row(gate_ref) * acc
    xmid_ref[...] = xm
    h = _rmsnorm(xm, g_ref[...]) * (1.0 + _row(scale_ref)) + _row(shift_ref)
    h_ref[...] = h
    logit_ref[...] = _dot3(h, wr_ref[...]) + br_ref[...]


def _post_mix(a_list, w_list, x, g, mod, w_route, b_route, name):
    t, d = x.shape
    tm = TOKEN_TILE
    n_in = len(a_list)
    row = lambda w: pl.BlockSpec((tm, w), lambda i: (i, 0))
    const = lambda shp: pl.BlockSpec(shp, lambda i: (0, 0))
    w_bytes = sum(w.size * 2 for w in w_list)
    return pl.pallas_call(
        functools.partial(_post_mix_kernel, n_in=n_in),
        out_shape=[jax.ShapeDtypeStruct((t, d), F32), jax.ShapeDtypeStruct((t, d), F32),
                   jax.ShapeDtypeStruct((t, LANES), F32)],
        grid=(t // tm,),
        in_specs=[row(a.shape[1]) for a in a_list] + [const(w.shape) for w in w_list]
        + [row(d), mod.spec(2, tm), const((1, d)), mod.spec(3, tm), mod.spec(4, tm), const((d, LANES)), const((1, LANES))],
        out_specs=[row(d), row(d), row(LANES)],
        compiler_params=_params(2 * w_bytes + 2 * tm * d * (2 * n_in + 12) + 8 * tm * d * 4, 1),
        name=name,
    )(*a_list, *w_list, x, mod.arr, g.reshape(1, d), mod.arr, mod.arr, w_route, b_route)


def _route_kernel(lg_ref, route_ref, cnt_ref, acc_ref):
    i = pl.program_id(0)
    tm = lg_ref.shape[0]

    @pl.when(i == 0)
    def _():
        acc_ref[...] = jnp.zeros_like(acc_ref)

    lg = lg_ref[...]
    lane = lax.broadcasted_iota(I32, lg.shape, 1)
    neg = -jnp.inf
    gl = jnp.where(lane < N_EXPERT_GROUPS, lg, neg)
    gmax = jnp.max(gl, axis=-1, keepdims=True)
    gidx = jnp.min(jnp.where(gl == gmax, lane, LANES), axis=-1, keepdims=True)
    g_top = 1.0 / jnp.sum(jnp.exp(gl - gmax), axis=-1, keepdims=True)
    lo = ROUTER_LANE0 + gidx * EXPERTS_PER_GROUP
    el = jnp.where((lane >= lo) & (lane < lo + EXPERTS_PER_GROUP), lg, neg)
    m1 = jnp.max(el, axis=-1, keepdims=True)
    i1 = jnp.min(jnp.where(el == m1, lane, LANES), axis=-1, keepdims=True)
    el2 = jnp.where(lane == i1, neg, el)
    m2 = jnp.max(el2, axis=-1, keepdims=True)
    i2 = jnp.min(jnp.where(el2 == m2, lane, LANES), axis=-1, keepdims=True)
    p2 = jnp.exp(m2 - m1)
    w1 = g_top / (1.0 + p2)
    w2 = w1 * p2
    sel1 = lane == i1
    sel2 = lane == i2
    onehot = jnp.where(sel1 | sel2, 1.0, 0.0).astype(BF16)
    r = lax.broadcasted_iota(I32, (tm, tm), 0)
    c = lax.broadcasted_iota(I32, (tm, tm), 1)
    tri = jnp.where(c <= r, 1.0, 0.0).astype(BF16)
    cum = _dot(tri, onehot) + acc_ref[...]
    r1 = jnp.sum(jnp.where(sel1, cum, 0.0), axis=-1, keepdims=True) - 1.0
    r2 = jnp.sum(jnp.where(sel2, cum, 0.0), axis=-1, keepdims=True) - 1.0
    acc_ref[...] = cum[tm - 1:tm, :]
    cnt_ref[...] = cum[tm - 1:tm, :]
    e1 = (i1 - ROUTER_LANE0).astype(F32)
    e2 = (i2 - ROUTER_LANE0).astype(F32)
    rec = jnp.zeros(lg.shape, F32)
    for k, val in enumerate((e1, e2, r1, r2, w1, w2)):
        rec = jnp.where(lane == k, val, rec)
    route_ref[...] = rec


def _route(logits):
    t = logits.shape[0]
    tm = TOKEN_TILE
    return pl.pallas_call(
        _route_kernel,
        out_shape=[jax.ShapeDtypeStruct((t, LANES), F32), jax.ShapeDtypeStruct((1, LANES), F32)],
        grid=(t // tm,),
        in_specs=[pl.BlockSpec((tm, LANES), lambda i: (i, 0))],
        out_specs=[pl.BlockSpec((tm, LANES), lambda i: (i, 0)), pl.BlockSpec((1, LANES), lambda i: (0, 0))],
        scratch_shapes=[pltpu.VMEM((1, LANES), F32)],
        compiler_params=_params(64 * tm * LANES * 4, 1),
        name="moe_route",
    )(logits)


def _row_copy(src_ref, src_row, dst_ref, dst_row, sem):
    return pltpu.make_async_copy(src_ref.at[pl.ds(src_row, 1)], dst_ref.at[pl.ds(dst_row, 1)], sem)


def _dispatch_kernel(dest_ref, h_ref, xs_in_ref, xs_ref, sem):
    del xs_in_ref
    tm = h_ref.shape[0]

    def start(r, _):
        for k in range(TOP_K):
            _row_copy(h_ref, r, xs_ref, dest_ref[TOP_K * r + k], sem).start(priority=k % 2)
        return 0

    lax.fori_loop(0, tm, start, 0, unroll=ROW_DMA_UNROLL)
    for k in range(TOP_K):
        pltpu.make_async_copy(h_ref, xs_ref.at[pl.ds(0, tm)], sem).wait()


def _dispatch(h, dest, xs):
    t, d = h.shape
    tm = min(ROW_DMA_TILE, t)
    return pl.pallas_call(
        _dispatch_kernel,
        out_shape=jax.ShapeDtypeStruct(xs.shape, xs.dtype),
        grid=(t // tm,),
        in_specs=[pl.BlockSpec((TOP_K * tm,), lambda i: (i,), memory_space=pltpu.SMEM),
                  pl.BlockSpec((tm, d), lambda i: (i, 0)),
                  pl.BlockSpec(memory_space=pl.ANY)],
        out_specs=pl.BlockSpec(memory_space=pl.ANY),
        scratch_shapes=[pltpu.SemaphoreType.DMA],
        input_output_aliases={2: 0},
        compiler_params=_params(4 * tm * d * 4, 1),
        name="moe_dispatch",
    )(dest, h, xs)


def _experts_kernel(te_ref, nu_ref, xs_ref, wg_ref, wu_ref, wd_ref, ys_ref):
    i = pl.program_id(0)

    @pl.when(i < nu_ref[0])
    def _():
        x = xs_ref[...].astype(BF16)
        hg = _dot(x, wg_ref[...].astype(BF16))
        hu = _dot(x, wu_ref[...].astype(BF16))
        hid = (_silu(hg) * hu).astype(BF16)
        ys_ref[...] = _dot(hid, wd_ref[...].astype(BF16))

    @pl.when(i >= nu_ref[0])
    def _():
        ys_ref[...] = jnp.zeros_like(ys_ref)


def _experts(xs, tile_expert, n_used, w_gate, w_up, w_down, layer):
    p, d = xs.shape
    tm = MOE_TILE
    f = w_gate.shape[-1]
    row_map = lambda i, te, nu: (jnp.minimum(i, jnp.maximum(nu[0] - 1, 0)), 0)
    return pl.pallas_call(
        _experts_kernel,
        out_shape=jax.ShapeDtypeStruct((p, d), F32),
        grid_spec=pltpu.PrefetchScalarGridSpec(
            num_scalar_prefetch=2,
            grid=(p // tm,),
            in_specs=[pl.BlockSpec((tm, d), row_map),
                      pl.BlockSpec((None, None, d, f), lambda i, te, nu: (layer, te[i], 0, 0)),
                      pl.BlockSpec((None, None, d, f), lambda i, te, nu: (layer, te[i], 0, 0)),
                      pl.BlockSpec((None, None, f, d), lambda i, te, nu: (layer, te[i], 0, 0))],
            out_specs=pl.BlockSpec((tm, d), lambda i, te, nu: (i, 0))),
        compiler_params=_params(2 * 3 * d * f * 4 + 4 * tm * d * 4 + 3 * d * f * 2 + 8 * tm * f * 4, 1),
        name="moe_experts",
    )(tile_expert, n_used, xs, w_gate, w_up, w_down)


def _combine_kernel(*refs, final):
    if final:
        dest_ref, route_ref, x_ref, gate_ref, fg_ref, ys_ref, o_ref, y_ref, gbuf_ref, sem = refs
    else:
        dest_ref, route_ref, x_ref, gate_ref, ys_ref, o_ref, gbuf_ref, sem = refs
    tm = x_ref.shape[0]

    def start(r, _):
        for k in range(TOP_K):
            _row_copy(ys_ref, dest_ref[TOP_K * r + k], gbuf_ref.at[k], r, sem).start(priority=k % 2)
        return 0

    lax.fori_loop(0, tm, start, 0, unroll=ROW_DMA_UNROLL)
    for k in range(TOP_K):
        pltpu.make_async_copy(ys_ref.at[pl.ds(0, tm)], gbuf_ref.at[k], sem).wait()
    route = route_ref[...]
    moe = gbuf_ref[0] * route[:, ROUTE_LANE_W:ROUTE_LANE_W + 1]
    for k in range(1, TOP_K):
        moe = moe + gbuf_ref[k] * route[:, ROUTE_LANE_W + k:ROUTE_LANE_W + k + 1]
    x = x_ref[...] + _row(gate_ref) * moe
    o_ref[...] = x
    if final:
        y_ref[...] = _rmsnorm(x, fg_ref[...])


def _combine(dest, route, x_mid, mod, ys, final_g):
    t, d = x_mid.shape
    tm = min(ROW_DMA_TILE, t)
    final = final_g is not None
    row = lambda w: pl.BlockSpec((tm, w), lambda i: (i, 0))
    in_specs = [pl.BlockSpec((TOP_K * tm,), lambda i: (i,), memory_space=pltpu.SMEM),
                row(LANES), row(d), mod.spec(5, tm)]
    args = [dest, route, x_mid, mod.arr]
    if final:
        in_specs.append(pl.BlockSpec((1, d), lambda i: (0, 0)))
        args.append(final_g.reshape(1, d))
    in_specs.append(pl.BlockSpec(memory_space=pl.ANY))
    args.append(ys)
    out_shape = [jax.ShapeDtypeStruct((t, d), F32)] * (2 if final else 1)
    out_specs = [row(d)] * (2 if final else 1)
    return pl.pallas_call(
        functools.partial(_combine_kernel, final=final),
        out_shape=out_shape,
        grid=(t // tm,),
        in_specs=in_specs,
        out_specs=out_specs,
        scratch_shapes=[pltpu.VMEM((TOP_K, tm, d), F32), pltpu.SemaphoreType.DMA],
        compiler_params=_params(8 * tm * d * 4 + TOP_K * tm * d * 4, 1),
        name="moe_combine",
    )(*args)


def _moe(layer, h_p, h_s, logit_p, logit_s, xmid_p, xmid_s, mod_p, mod_s, w_gate, w_up, w_down, final_g):
    tp, d = h_p.shape
    ts = h_s.shape[0]
    route, counts = _route(jnp.concatenate([logit_p, logit_s], axis=0))
    cnt = counts[0, ROUTER_LANE0:ROUTER_LANE0 + N_EXPERTS].astype(I32)
    padded = (cnt + MOE_TILE - 1) // MOE_TILE * MOE_TILE
    ends = jnp.cumsum(padded)
    starts = ends - padded
    expert = route[:, ROUTE_LANE_E:ROUTE_LANE_E + TOP_K].astype(I32)
    rank = route[:, ROUTE_LANE_R:ROUTE_LANE_R + TOP_K].astype(I32)
    dest = (starts[expert] + rank).reshape(-1)
    n_rows = (tp + ts) * TOP_K + N_EXPERTS * MOE_TILE
    n_tiles = n_rows // MOE_TILE
    n_used = (ends[-1:] // MOE_TILE).astype(I32)
    tile_start = jnp.arange(n_tiles, dtype=I32) * MOE_TILE
    tile_expert = jnp.minimum(jnp.sum((ends[None, :] <= tile_start[:, None]).astype(I32), axis=1), N_EXPERTS - 1)
    xs = jnp.zeros((n_rows, d), F32)
    xs = _dispatch(h_p, dest[:tp * TOP_K], xs)
    xs = _dispatch(h_s, dest[tp * TOP_K:], xs)
    ys = _experts(xs, tile_expert, n_used, w_gate, w_up, w_down, layer)
    out_p = _combine(dest[:tp * TOP_K], route[:tp], xmid_p, mod_p, ys, final_g)
    out_s = _combine(dest[tp * TOP_K:], route[tp:], xmid_s, mod_s, ys, final_g)
    return out_p, out_s


def kernel(x_prompt, x_sample, c_prompt, c_sample, state_ssm, state_ssd_conv, cache_k, cache_v, page_table, state_short_conv, ada_w, ada_b, ln_mix_g, ln_ffn_g, mix_w_in, mix_w_out, ssd_conv_w, ssd_conv_b, ssd_dt_bias, ssd_a_log, ssd_d, ssd_norm_g, sb_bias, sc_w_in, sc_conv_w, sc_w_out, moe_w_group, moe_b_group, moe_w_router, moe_b_router, moe_w_gate, moe_w_up, moe_w_down, final_g):
    b, s, d = x_prompt.shape
    bs, ls, _ = x_sample.shape
    tp, ts = b * s, bs * ls
    depth = ada_w.shape[0]
    xp = x_prompt.reshape(tp, d)
    xs = x_sample.reshape(ts, d)
    mod = _adaln(jnp.concatenate([c_prompt, c_sample], axis=0), ada_w, ada_b)
    ls_pad = SUBLANES

    def pad_rows(a, rows, front=0):
        return jnp.pad(a, ((0, 0), (front, rows - a.shape[1] - front), (0, 0)))

    ssm_p, ssm_s, cv_p, cv_s, k_p, k_s, v_p, v_s, sc_p, sc_s = ([] for _ in range(10))
    y_out = None
    for layer in range(depth):
        i = layer // 2
        mod_p = _Mod(mod[layer, :b].reshape(b, 1, 6 * d), s, d)
        mod_s = _Mod(jnp.repeat(mod[layer, b:], ls, axis=0), ls, d)
        n_route = N_EXPERT_GROUPS + N_EXPERTS
        w_route = jnp.pad(jnp.concatenate([moe_w_group[layer], moe_w_router[layer]], axis=1),
                          ((0, 0), (0, LANES - n_route)))
        b_route = jnp.pad(jnp.concatenate([moe_b_group[layer], moe_b_router[layer]]),
                          (0, LANES - n_route)).reshape(1, LANES)
        if layer % 2 == 0:
            w = mix_w_in[i]
            o_dt = D_SSM + CONV_DIM
            o_q = o_dt + SSM_HEADS
            w_bf = jnp.concatenate(
                [w[:, :o_dt], w[:, o_q:o_q + D_ATT] * SB_SCALE, w[:, o_q + D_ATT:], w[:, o_dt:o_q],
                 jnp.zeros((d, LANES - SSM_HEADS), F32)], axis=1).astype(BF16)
            c0 = D_SSM + CONV_DIM
            splits = ((0, D_SSM, (F32,)), (D_SSM, CONV_DIM, (F32,)), (c0, D_ATT, (BF16,)),
                      (c0 + D_ATT, D_ATT, (F32, BF16)), (c0 + 2 * D_ATT, D_ATT, (F32, BF16)),
                      (c0 + 3 * D_ATT, LANES, (F32,)))
            z1, xbc1, q1, kf1, kb1, vf1, vb1, dt1 = _pre_mix(xp, ln_mix_g[layer], mod_p, w_bf, splits, "pre_mix_even_p")
            z2, xbc2, q2, kf2, kb2, vf2, vb2, dt2 = _pre_mix(xs, ln_mix_g[layer], mod_s, w_bf, splits, "pre_mix_even_s")
            prm = (ssd_conv_w[i], ssd_conv_b[i].reshape(1, CONV_DIM),
                   jnp.pad(ssd_dt_bias[i], (0, LANES - SSM_HEADS)).reshape(1, LANES),
                   jnp.pad(ssd_a_log[i], (0, LANES - SSM_HEADS)).reshape(1, LANES),
                   jnp.repeat(ssd_d[i], SSM_HEAD_DIM).reshape(1, D_SSM),
                   ssd_norm_g[i].reshape(1, D_SSM))
            y1, hfin1, cv1 = _ssd(xbc1, z1, dt1, prm, b, SSD_CHUNK, SSD_CHUNK, name="ssd_prompt")
            seq_pad = lambda a: pad_rows(a.reshape(bs, ls, a.shape[-1]), ls_pad).reshape(bs * ls_pad, a.shape[-1])
            y2, hfin2, cv2 = _ssd(
                seq_pad(xbc2), seq_pad(z2), seq_pad(dt2), prm, bs, ls_pad, ls,
                tail0=pad_rows(state_ssd_conv[i], SUBLANES, front=SUBLANES - (SSD_CONV - 1)),
                h0=state_ssm[i].reshape(bs, D_SSM, D_STATE), name="ssd_sample")
            y2 = y2.reshape(bs, ls_pad, D_SSM)[:, :ls].reshape(ts, D_SSM)
            att1 = _attn_prompt(q1, kb1, vb1, sb_bias[i], b)
            by_head = lambda a: a.reshape(bs, ls * ATT_HEADS, ATT_HEAD_DIM)
            att2 = _attn_sample(by_head(q2), by_head(kf2), by_head(vf2), cache_k, cache_v, i,
                                page_table, sb_bias[i]).reshape(ts, D_ATT)
            w_out = mix_w_out[i].astype(BF16)
            w_list = [w_out[:D_SSM], w_out[D_SSM:]]
            xm1, hf1, lg1 = _post_mix([y1, att1], w_list, xp, ln_ffn_g[layer], mod_p, w_route, b_route, "post_mix_even_p")
            xm2, hf2, lg2 = _post_mix([y2, att2], w_list, xs, ln_ffn_g[layer], mod_s, w_route, b_route, "post_mix_even_s")
            ssm_p.append(hfin1.reshape(b, SSM_HEADS, SSM_HEAD_DIM, D_STATE))
            ssm_s.append(hfin2.reshape(bs, SSM_HEADS, SSM_HEAD_DIM, D_STATE))
            cv_p.append(cv1)
            cv_s.append(cv2)
            k_p.append(kf1.reshape(b, s, ATT_HEADS, ATT_HEAD_DIM))
            k_s.append(kf2.reshape(bs, ls, ATT_HEADS, ATT_HEAD_DIM))
            v_p.append(vf1.reshape(b, s, ATT_HEADS, ATT_HEAD_DIM))
            v_s.append(vf2.reshape(bs, ls, ATT_HEADS, ATT_HEAD_DIM))
        else:
            w_bf = sc_w_in[i].astype(BF16)
            splits = tuple((j * d, d, (F32,)) for j in range(3))
            bg1, cg1, u1 = _pre_mix(xp, ln_mix_g[layer], mod_p, w_bf, splits, "pre_mix_odd_p")
            bg2, cg2, u2 = _pre_mix(xs, ln_mix_g[layer], mod_s, w_bf, splits, "pre_mix_odd_s")
            a1, st1 = _sc_prompt(bg1, cg1, u1, sc_conv_w[i], b)
            a2, st2 = _sc_sample(bg2, cg2, u2, state_short_conv[i], sc_conv_w[i], bs)
            w_list = [sc_w_out[i].astype(BF16)]
            xm1, hf1, lg1 = _post_mix([a1], w_list, xp, ln_ffn_g[layer], mod_p, w_route, b_route, "post_mix_odd_p")
            xm2, hf2, lg2 = _post_mix([a2], w_list, xs, ln_ffn_g[layer], mod_s, w_route, b_route, "post_mix_odd_s")
            sc_p.append(st1)
            sc_s.append(st2)
        fg = final_g if layer == depth - 1 else None
        out_p, out_s = _moe(layer, hf1, hf2, lg1, lg2, xm1, xm2, mod_p, mod_s,
                            moe_w_gate, moe_w_up, moe_w_down, fg)
        xp, xs = out_p[0], out_s[0]
        if fg is not None:
            y_out = (out_p[1].reshape(b, s, d), out_s[1].reshape(bs, ls, d))
    return (y_out[0], y_out[1], jnp.stack(ssm_p), jnp.stack(ssm_s), jnp.stack(cv_p), jnp.stack(cv_s),
            jnp.stack(k_p), jnp.stack(k_s), jnp.stack(v_p), jnp.stack(v_s), jnp.stack(sc_p), jnp.stack(sc_s))
```

```python
import functools

import jax
import jax.numpy as jnp
from jax import lax
from jax.experimental import pallas as pl
from jax.experimental.pallas import tpu as pltpu

F32 = jnp.float32
BF16 = jnp.bfloat16
I32 = jnp.int32

SSM_HEADS = 16
SSM_HEAD_DIM = 64
D_SSM = SSM_HEADS * SSM_HEAD_DIM
SSM_GROUPS = 2
HEADS_PER_GROUP = SSM_HEADS // SSM_GROUPS
D_STATE = 128
SSD_CONV = 4
SSD_CHUNK = 128
CONV_DIM = D_SSM + 2 * SSM_GROUPS * D_STATE
ATT_HEADS = 16
ATT_HEAD_DIM = 64
D_ATT = ATT_HEADS * ATT_HEAD_DIM
SB_SCALE = ATT_HEAD_DIM ** -0.5
SC_WIDTH = 3
N_EXPERT_GROUPS = 4
EXPERTS_PER_GROUP = 8
N_EXPERTS = N_EXPERT_GROUPS * EXPERTS_PER_GROUP
TOP_K = 2
D_EXPERT = 256
RMS_EPS = 1e-6

LANES = 128
SUBLANES = 8
VMEM_BYTES_V7X = 64 * 1024 * 1024
VMEM_LIMIT_CAP = VMEM_BYTES_V7X - 8 * 1024 * 1024

TOKEN_TILE = 256
ATT_TILE = 256
ATT_HEAD_BLOCK = 4
SAMPLE_PAGES_PER_STEP = 8
MOE_TILE = 256
ROW_DMA_TILE = 512
ROW_DMA_UNROLL = 8
ROUTE_LANE_E = 0
ROUTE_LANE_R = 2
ROUTE_LANE_W = 4
ROUTER_LANE0 = N_EXPERT_GROUPS


def _vmem_limit(nbytes):
    return int(min(max(2 * nbytes, 32 * 1024 * 1024), VMEM_LIMIT_CAP))


def _params(nbytes, n_axes):
    return pltpu.CompilerParams(dimension_semantics=("arbitrary",) * n_axes,
                                vmem_limit_bytes=_vmem_limit(nbytes))


def _dot(a, b):
    return jnp.dot(a, b, preferred_element_type=F32)


def _dot_nt(a, b):
    return lax.dot_general(a, b, (((1,), (1,)), ((), ())), preferred_element_type=F32)


def _dot_tn(a, b):
    return lax.dot_general(a, b, (((0,), (0,)), ((), ())), preferred_element_type=F32)


def _split2(x):
    hi = x.astype(BF16)
    lo = (x - hi.astype(F32)).astype(BF16)
    return hi, lo


def _split3(x):
    hi = x.astype(BF16)
    r = x - hi.astype(F32)
    mid = r.astype(BF16)
    lo = (r - mid.astype(F32)).astype(BF16)
    return hi, mid, lo


def _dot3(a, b):
    ah, al = _split2(a)
    bh, bl = _split2(b)
    return _dot(ah, bh) + (_dot(al, bh) + _dot(ah, bl))


def _silu(x):
    return x / (1.0 + jnp.exp(-x))


def _softplus(x):
    return jnp.maximum(x, 0.0) + jnp.log(1.0 + jnp.exp(-jnp.abs(x)))


def _rmsnorm(x, g):
    return x * lax.rsqrt(jnp.mean(x * x, axis=-1, keepdims=True) + RMS_EPS) * g


def _row(ref):
    v = ref[...]
    return v[0] if v.ndim == 3 else v


def _adaln_kernel(c_ref, w_ref, b_ref, o_ref):
    o_ref[...] = _dot3(_silu(c_ref[...]), w_ref[...]) + b_ref[...]


def _adaln(c_all, ada_w, ada_b):
    depth, d, n = ada_w.shape
    rows = c_all.shape[0]
    tn = 1024
    return pl.pallas_call(
        _adaln_kernel,
        out_shape=jax.ShapeDtypeStruct((depth, rows, n), F32),
        grid=(depth, n // tn),
        in_specs=[pl.BlockSpec((rows, d), lambda l, j: (0, 0)),
                  pl.BlockSpec((None, d, tn), lambda l, j: (l, 0, j)),
                  pl.BlockSpec((None, 1, tn), lambda l, j: (l, 0, j))],
        out_specs=pl.BlockSpec((None, rows, tn), lambda l, j: (l, 0, j)),
        compiler_params=_params(2 * d * tn * 4 + 4 * rows * (d + tn) * 4, 2),
        name="adaln",
    )(c_all, ada_w, ada_b.reshape(depth, 1, n))


class _Mod:
    def __init__(self, arr, seq_len, d):
        self.arr = arr
        self.seq_len = seq_len
        self.d = d

    def spec(self, which, tm):
        d = self.d
        if self.arr.ndim == 3:
            tiles_per_seq = self.seq_len // tm
            return pl.BlockSpec((1, 1, d), lambda i, *_: (i // tiles_per_seq, 0, which))
        return pl.BlockSpec((tm, d), lambda i, *_: (i, which))


def _pre_mix_kernel(x_ref, g_ref, shift_ref, scale_ref, w_ref, *out_refs, splits):
    h = (_rmsnorm(x_ref[...], g_ref[...]) * (1.0 + _row(scale_ref)) + _row(shift_ref)).astype(BF16)
    k = 0
    for start, width, dtypes in splits:
        acc = _dot(h, w_ref[:, start:start + width])
        for dt in dtypes:
            out_refs[k][...] = acc.astype(dt)
            k += 1


def _pre_mix(x, g, mod, w_bf, splits, name):
    t, d = x.shape
    n = w_bf.shape[1]
    tm = TOKEN_TILE
    out_shape, out_specs, out_bytes = [], [], 0
    for _, width, dtypes in splits:
        for dt in dtypes:
            out_shape.append(jax.ShapeDtypeStruct((t, width), dt))
            out_specs.append(pl.BlockSpec((tm, width), lambda i: (i, 0)))
            out_bytes += tm * width * jnp.dtype(dt).itemsize
    return pl.pallas_call(
        functools.partial(_pre_mix_kernel, splits=splits),
        out_shape=out_shape,
        grid=(t // tm,),
        in_specs=[pl.BlockSpec((tm, d), lambda i: (i, 0)),
                  pl.BlockSpec((1, d), lambda i: (0, 0)),
                  mod.spec(0, tm), mod.spec(1, tm),
                  pl.BlockSpec((d, n), lambda i: (0, 0))],
        out_specs=out_specs,
        compiler_params=_params(2 * d * n * 2 + 2 * tm * d * 4 + 2 * out_bytes + tm * n * 4, 1),
        name=name,
    )(x, g.reshape(1, d), mod.arr, mod.arr, w_bf)


def _ssd_kernel(*refs, chunk, n_real, has_init):
    if has_init:
        (xbc_ref, z_ref, dt_ref, tail0_ref, h0_ref, cw_ref, cb_ref, dtb_ref, alog_ref, dsk_ref, ng_ref,
         y_ref, hout_ref, cvout_ref, buf_ref, state_ref, ybuf_ref) = refs
    else:
        (xbc_ref, z_ref, dt_ref, cw_ref, cb_ref, dtb_ref, alog_ref, dsk_ref, ng_ref,
         y_ref, hout_ref, cvout_ref, buf_ref, state_ref, ybuf_ref) = refs
    c = pl.program_id(1)
    last = pl.num_programs(1) - 1
    L = chunk
    T0 = SUBLANES

    @pl.when(c == 0)
    def _():
        if has_init:
            buf_ref[0:T0, :] = tail0_ref[0]
            state_ref[...] = h0_ref[0]
        else:
            buf_ref[0:T0, :] = jnp.zeros((T0, CONV_DIM), F32)
            state_ref[...] = jnp.zeros_like(state_ref)

    xbc = xbc_ref[...]
    buf_ref[T0:T0 + L, :] = xbc
    x_ext = buf_ref[0:T0 + L, :]
    conv = xbc * cw_ref[SSD_CONV - 1:SSD_CONV, :]
    for j in range(SSD_CONV - 1):
        shifted = pltpu.roll(x_ext, SSD_CONV - 1 - j, axis=0)[T0:]
        conv = conv + shifted * cw_ref[j:j + 1, :]
    xc = _silu(conv + cb_ref[...])
    xs = xc[:, :D_SSM]
    bmat = xc[:, D_SSM:D_SSM + SSM_GROUPS * D_STATE].astype(BF16)
    cmat = xc[:, D_SSM + SSM_GROUPS * D_STATE:].astype(BF16)

    @pl.when(c == last)
    def _():
        cvout_ref[0] = buf_ref[pl.ds(T0 + n_real - (SSD_CONV - 1), SSD_CONV - 1), :]

    if n_real == L:
        buf_ref[0:T0, :] = buf_ref[L:L + T0, :]

    dt = _softplus(dt_ref[...] + dtb_ref[...])
    rows = lax.broadcasted_iota(I32, (L, L), 0)
    cols = lax.broadcasted_iota(I32, (L, L), 1)
    if n_real < L:
        dt = jnp.where(lax.broadcasted_iota(I32, dt.shape, 0) < n_real, dt, 0.0)
    a = -jnp.exp(alog_ref[...])
    tri = cols <= rows
    tri_bf = jnp.where(tri, 1.0, 0.0).astype(BF16)
    da_hi, da_mid, da_lo = _split3(dt * a)
    acum = _dot(tri_bf, da_hi) + (_dot(tri_bf, da_mid) + _dot(tri_bf, da_lo))
    acum_t = acum.T
    a_last = acum[L - 1:L, :]
    chunk_decay = jnp.exp(a_last)
    per_head = jnp.concatenate([dt, jnp.exp(acum), jnp.exp(a_last - acum)], axis=0)
    head_of_lane = lax.broadcasted_iota(I32, (3 * LANES, D_SSM), 1) // SSM_HEAD_DIM
    expand3 = jnp.where(lax.broadcasted_iota(I32, (3 * LANES, D_SSM), 0) % LANES == head_of_lane, 1.0, 0.0).astype(BF16)
    wide = _dot(jnp.concatenate(_split3(per_head), axis=1), expand3)
    dt_w, e_acum_w, to_end_w = wide[:L], wide[L:2 * L], wide[2 * L:]
    xdt = xs * dt_w
    xdt_bf = xdt.astype(BF16)
    x_end = (xdt * to_end_w).astype(BF16)

    gw = HEADS_PER_GROUP * SSM_HEAD_DIM
    for g in range(SSM_GROUPS):
        gl = slice(g * gw, (g + 1) * gw)
        gs = slice(g * D_STATE, (g + 1) * D_STATE)
        st = state_ref[gl, :]
        ybuf_ref[:, gl] = _dot_nt(cmat[:, gs], st.astype(BF16)) * e_acum_w[:, gl] + xs[:, gl] * dsk_ref[:, gl]
        s_new = _dot_tn(x_end[:, gl], bmat[:, gs])
        for h in range(g * HEADS_PER_GROUP, (g + 1) * HEADS_PER_GROUP):
            hs = slice(h * SSM_HEAD_DIM, (h + 1) * SSM_HEAD_DIM)
            hg = slice(hs.start - gl.start, hs.stop - gl.start)
            state_ref[hs, :] = st[hg] * chunk_decay[:, h:h + 1] + s_new[hg]

    cb = [_dot_nt(cmat[:, g * D_STATE:(g + 1) * D_STATE], bmat[:, g * D_STATE:(g + 1) * D_STATE])
          for g in range(SSM_GROUPS)]
    first_half = lax.broadcasted_iota(I32, (1, LANES), 1) < SSM_HEAD_DIM
    for pair in range(SSM_HEADS // 2):
        g = (2 * pair) // HEADS_PER_GROUP
        pl_ = slice(pair * LANES, (pair + 1) * LANES)
        ms = []
        for h in (2 * pair, 2 * pair + 1):
            seg = acum[:, h:h + 1] - acum_t[h:h + 1, :]
            ms.append((cb[g] * jnp.exp(jnp.where(tri, seg, -jnp.inf))).astype(BF16))
        xp = xdt_bf[:, pl_]
        zero = jnp.zeros_like(xp)
        x_bd = jnp.concatenate([jnp.where(first_half, xp, zero), jnp.where(first_half, zero, xp)], axis=0)
        ybuf_ref[:, pl_] += _dot(jnp.concatenate(ms, axis=1), x_bd)

    y = ybuf_ref[...] * _silu(z_ref[...])
    half = D_SSM // SSM_GROUPS
    for g in range(SSM_GROUPS):
        yg = y[:, g * half:(g + 1) * half]
        y_ref[:, g * half:(g + 1) * half] = (
            yg * lax.rsqrt(jnp.mean(yg * yg, axis=-1, keepdims=True) + RMS_EPS) * ng_ref[:, g * half:(g + 1) * half]
        ).astype(y_ref.dtype)

    @pl.when(c == last)
    def _():
        hout_ref[0] = state_ref[...]


def _ssd(xbc, z, dt, prm, n_seq, chunk, n_real, tail0=None, h0=None, name="ssd"):
    t = xbc.shape[0]
    nc = t // (n_seq * chunk)
    has_init = tail0 is not None
    cw, cb, dtb, alog, dsk, ng = prm
    row_spec = lambda w: pl.BlockSpec((chunk, w), lambda b, c: (b * nc + c, 0))
    const = lambda shp: pl.BlockSpec(shp, lambda b, c: (0,) * len(shp))
    in_specs = [row_spec(CONV_DIM), row_spec(D_SSM), row_spec(LANES)]
    args = [xbc, z, dt]
    if has_init:
        in_specs += [pl.BlockSpec((1, SUBLANES, CONV_DIM), lambda b, c: (b, 0, 0)),
                     pl.BlockSpec((1, D_SSM, D_STATE), lambda b, c: (b, 0, 0))]
        args += [tail0, h0]
    in_specs += [const((SSD_CONV, CONV_DIM)), const((1, CONV_DIM)), const((1, LANES)), const((1, LANES)),
                 const((1, D_SSM)), const((1, D_SSM))]
    args += [cw, cb, dtb, alog, dsk, ng]
    scratch_bytes = ((chunk + 2 * SUBLANES) * CONV_DIM + D_SSM * D_STATE + chunk * D_SSM) * 4
    block_bytes = chunk * (CONV_DIM + D_SSM + LANES) * 4 + chunk * D_SSM * 2 + 2 * D_SSM * D_STATE * 4
    return pl.pallas_call(
        functools.partial(_ssd_kernel, chunk=chunk, n_real=n_real, has_init=has_init),
        out_shape=[jax.ShapeDtypeStruct((t, D_SSM), BF16),
                   jax.ShapeDtypeStruct((n_seq, D_SSM, D_STATE), F32),
                   jax.ShapeDtypeStruct((n_seq, SSD_CONV - 1, CONV_DIM), F32)],
        grid=(n_seq, nc),
        in_specs=in_specs,
        out_specs=[row_spec(D_SSM),
                   pl.BlockSpec((1, D_SSM, D_STATE), lambda b, c: (b, 0, 0)),
                   pl.BlockSpec((1, SSD_CONV - 1, CONV_DIM), lambda b, c: (b, 0, 0))],
        scratch_shapes=[pltpu.VMEM((chunk + 2 * SUBLANES, CONV_DIM), F32),
                        pltpu.VMEM((D_SSM, D_STATE), F32),
                        pltpu.VMEM((chunk, D_SSM), F32)],
        compiler_params=_params(2 * block_bytes + scratch_bytes + 8 * chunk * CONV_DIM * 4, 2),
        name=name,
    )(*args)


MASKED_LOGIT = -1e30


def _sb_scores(s, neg_upper2):
    sp = _softplus(s)
    hi, lo = _split2(sp)
    later = _dot(jnp.concatenate([hi, lo], axis=1), neg_upper2)
    return s - sp, later, later[:, 0:1] - sp[:, 0:1]


def _neg_strict_upper2(n):
    j = lax.broadcasted_iota(I32, (2 * n, n), 0) % n
    k = lax.broadcasted_iota(I32, (2 * n, n), 1)
    return jnp.where(j > k, -1.0, 0.0).astype(BF16)


def _attn_prompt_kernel(bias_ref, q_ref, k_ref, v_ref, o_ref, qs_ref, carry_ref, acc_ref):
    hq = pl.program_id(1)
    qi = pl.program_id(2)
    tq, width = q_ref.shape
    tk = ATT_TILE
    nh = ATT_HEAD_BLOCK
    m = nh * tq
    lane_head = lax.broadcasted_iota(I32, (1, width), 1) // ATT_HEAD_DIM
    q = q_ref[...]
    for h in range(nh):
        qs_ref[h * tq:(h + 1) * tq, :] = jnp.where(lane_head == h, q, jnp.zeros_like(q))
    row_head = lax.broadcasted_iota(I32, (m, 1), 0) // tq
    bias = jnp.zeros((m, 1), F32)
    for h in range(nh):
        bias = jnp.where(row_head == h, bias_ref[hq * nh + h], bias)
    neg_upper2 = _neg_strict_upper2(tk)
    causal = lax.broadcasted_iota(I32, (m, tk), 1) < lax.broadcasted_iota(I32, (m, tk), 0) % tq

    def tile(j, mask):
        start = pl.multiple_of(j * tk, tk)
        vt = v_ref[pl.ds(start, tk), :]
        s = _dot_nt(qs_ref[...], k_ref[pl.ds(start, tk), :]) + bias
        if mask is not None:
            s = jnp.where(mask, s, MASKED_LOGIT)
        log_beta, later, total = _sb_scores(s, neg_upper2)
        carry = carry_ref[...]
        w = jnp.exp(log_beta + (later + carry)).astype(BF16)
        carry_ref[...] = carry + total
        w_cat = jnp.concatenate([w[h * tq:(h + 1) * tq] for h in range(nh)], axis=1)
        v_cat = jnp.concatenate([jnp.where(lane_head == h, vt, jnp.zeros_like(vt)) for h in range(nh)], axis=0)
        acc_ref[...] += _dot(w_cat, v_cat)

    carry_ref[...] = jnp.zeros_like(carry_ref)
    acc_ref[...] = jnp.zeros_like(acc_ref)
    tile(qi, causal)

    def body(i, c):
        tile(qi - 1 - i, None)
        return c

    lax.fori_loop(0, qi, body, 0)
    o_ref[...] = acc_ref[...].astype(o_ref.dtype)


def _attn_prompt(q, k, v, bias, n_seq):
    t, d = q.shape
    s = t // n_seq
    tq = ATT_TILE
    nq = s // tq
    width = ATT_HEAD_BLOCK * ATT_HEAD_DIM
    return pl.pallas_call(
        _attn_prompt_kernel,
        out_shape=jax.ShapeDtypeStruct((t, d), BF16),
        grid_spec=pltpu.PrefetchScalarGridSpec(
            num_scalar_prefetch=1,
            grid=(n_seq, d // width, nq),
            in_specs=[pl.BlockSpec((tq, width), lambda b, hq, qi, *_: (b * nq + qi, hq)),
                      pl.BlockSpec((s, width), lambda b, hq, qi, *_: (b, hq)),
                      pl.BlockSpec((s, width), lambda b, hq, qi, *_: (b, hq))],
            out_specs=pl.BlockSpec((tq, width), lambda b, hq, qi, *_: (b * nq + qi, hq)),
            scratch_shapes=[pltpu.VMEM((ATT_HEAD_BLOCK * tq, width), BF16),
                            pltpu.VMEM((ATT_HEAD_BLOCK * tq, 1), F32),
                            pltpu.VMEM((tq, width), F32)]),
        compiler_params=_params(4 * s * width * 2 + 4 * tq * width * 2
                                + 12 * ATT_HEAD_BLOCK * tq * ATT_TILE * 4, 3),
        name="sb_attn_prompt",
    )(bias, q, k, v)


def _attn_sample_kernel(pt_ref, bias_ref, q_ref, kn_ref, vn_ref, *refs, n_pages, n_new):
    k_refs = refs[:n_pages]
    v_refs = refs[n_pages:2 * n_pages]
    o_ref = refs[2 * n_pages]
    d = q_ref.shape[-1]
    page = k_refs[0].shape[-1]
    n_rows = n_new * ATT_HEADS
    n_blocks = n_pages + 1
    m = n_blocks * n_rows
    row_head = lax.broadcasted_iota(I32, (n_rows, 1), 0) % ATT_HEADS
    lane_head = lax.broadcasted_iota(I32, (1, d), 1) // ATT_HEAD_DIM
    own = lane_head == row_head
    q = q_ref[0]
    qx = jnp.broadcast_to(q[:, None, :], (n_new, ATT_HEADS, d)).reshape(n_rows, d)
    qx = jnp.where(own, qx, jnp.zeros_like(qx))
    bias = jnp.zeros((n_rows, 1), F32)
    for h in range(ATT_HEADS):
        bias = jnp.where(row_head == h, bias_ref[h], bias)

    pad = jnp.zeros((page - kn_ref.shape[1], d), F32)
    k_new = jnp.concatenate([kn_ref[0], pad], axis=0).astype(BF16)
    v_new = jnp.concatenate([vn_ref[0], pad], axis=0).astype(BF16)
    order = range(n_pages - 1, -1, -1)
    kt = jnp.concatenate([k_refs[p][...].reshape(d, page).astype(BF16) for p in order], axis=1)
    vt = jnp.concatenate([v_refs[p][...].reshape(d, page).astype(BF16) for p in order], axis=1)

    s_pages = _dot(qx, kt)
    s = jnp.concatenate([_dot_nt(qx, k_new)] + [s_pages[:, i * page:(i + 1) * page] for i in range(n_pages)], axis=0)
    s = s + jnp.concatenate([bias] * n_blocks, axis=0)
    row = lax.broadcasted_iota(I32, (m, page), 0)
    col = lax.broadcasted_iota(I32, (m, page), 1)
    s = jnp.where((row >= n_rows) | (col < row // ATT_HEADS), s, MASKED_LOGIT)
    log_beta, later, total = _sb_scores(s, _neg_strict_upper2(page))
    carry = jnp.zeros((n_rows, 1), F32)
    carries = []
    for i in range(n_blocks):
        carries.append(carry)
        carry = carry + total[i * n_rows:(i + 1) * n_rows]
    w = jnp.exp(log_beta + (later + jnp.concatenate(carries, axis=0))).astype(BF16)
    w_pages = jnp.concatenate([w[(i + 1) * n_rows:(i + 2) * n_rows] for i in range(n_pages)], axis=1)
    acc = _dot(w[:n_rows], v_new) + _dot_nt(w_pages, vt)
    acc = jnp.where(own, acc, 0.0)
    o_ref[0] = jnp.sum(acc.reshape(n_new, ATT_HEADS, d), axis=1).astype(o_ref.dtype)


def _attn_sample(q, k_new, v_new, cache_kt, cache_vt, layer, page_table, bias):
    bs, n_new, d = q.shape
    n_pages = page_table.shape[1]
    _, _, n_heads, dh, page = cache_kt.shape
    page_spec = lambda p: pl.BlockSpec((None, None, n_heads, dh, page),
                                       lambda b, pt, bias_: (layer, pt[b, p], 0, 0, 0))
    seq_spec = lambda rows: pl.BlockSpec((1, rows, d), lambda b, pt, bias_: (b, 0, 0))
    page_bytes = page * d * 4
    return pl.pallas_call(
        functools.partial(_attn_sample_kernel, n_pages=n_pages, n_new=n_new),
        out_shape=jax.ShapeDtypeStruct((bs, n_new, d), BF16),
        grid_spec=pltpu.PrefetchScalarGridSpec(
            num_scalar_prefetch=2,
            grid=(bs,),
            in_specs=[seq_spec(n_new), seq_spec(k_new.shape[1]), seq_spec(v_new.shape[1])]
            + [page_spec(p) for p in range(n_pages)] * 2,
            out_specs=seq_spec(n_new)),
        compiler_params=_params(2 * 2 * n_pages * page_bytes + 8 * page_bytes, 1),
        name="sb_attn_sample",
    )(page_table, bias, q, k_new, v_new, *([cache_kt] * n_pages), *([cache_vt] * n_pages))


def _sc_prompt_kernel(b_ref, c_ref, u_ref, w_ref, a_ref, st_ref, buf_ref):
    L = c_ref.shape[0]
    T0 = SUBLANES
    cu = c_ref[...] * u_ref[...]
    buf_ref[0:T0, :] = jnp.zeros((T0, cu.shape[1]), F32)
    buf_ref[T0:T0 + L, :] = cu
    conv = cu * w_ref[SC_WIDTH - 1:SC_WIDTH, :]
    for j in range(SC_WIDTH - 1):
        conv = conv + buf_ref[pl.ds(T0 - (SC_WIDTH - 1) + j, L), :] * w_ref[j:j + 1, :]
    a_ref[...] = (b_ref[...] * conv).astype(a_ref.dtype)
    st_ref[0] = buf_ref[pl.ds(T0 + L - (SC_WIDTH - 1), SC_WIDTH - 1), :]


def _sc_prompt(bg, cg, u, conv_w, n_seq):
    t, d = cg.shape
    s = t // n_seq
    wl = 256
    spec = pl.BlockSpec((s, wl), lambda b, j: (b, j))
    return pl.pallas_call(
        _sc_prompt_kernel,
        out_shape=[jax.ShapeDtypeStruct((t, d), BF16),
                   jax.ShapeDtypeStruct((n_seq, SC_WIDTH - 1, d), F32)],
        grid=(n_seq, d // wl),
        in_specs=[spec, spec, spec, pl.BlockSpec((SC_WIDTH, wl), lambda b, j: (0, j))],
        out_specs=[spec, pl.BlockSpec((1, SC_WIDTH - 1, wl), lambda b, j: (b, 0, j))],
        scratch_shapes=[pltpu.VMEM((s + SUBLANES, wl), F32)],
        compiler_params=_params(2 * 3 * s * wl * 4 + 2 * s * wl * 2 + 6 * s * wl * 4, 2),
        name="short_conv_prompt",
    )(bg, cg, u, conv_w)


def _sc_sample_kernel(b_ref, c_ref, u_ref, st_ref, w_ref, a_ref, sto_ref, *, n_new, d):
    up = [st_ref[:, j * d:(j + 1) * d] for j in range(SC_WIDTH - 1)]
    up += [c_ref[:, t * d:(t + 1) * d] * u_ref[:, t * d:(t + 1) * d] for t in range(n_new)]
    for t in range(n_new):
        conv = up[t] * w_ref[0:1, :]
        for j in range(1, SC_WIDTH):
            conv = conv + up[t + j] * w_ref[j:j + 1, :]
        a_ref[:, t * d:(t + 1) * d] = (b_ref[:, t * d:(t + 1) * d] * conv).astype(a_ref.dtype)
    for j in range(SC_WIDTH - 1):
        sto_ref[:, j * d:(j + 1) * d] = up[n_new + j]


def _sc_sample(bg, cg, u, state, conv_w, n_seq):
    t, d = cg.shape
    n_new = t // n_seq
    wide = lambda x: x.reshape(n_seq, n_new * d)
    a, st = pl.pallas_call(
        functools.partial(_sc_sample_kernel, n_new=n_new, d=d),
        out_shape=[jax.ShapeDtypeStruct((n_seq, n_new * d), BF16),
                   jax.ShapeDtypeStruct((n_seq, (SC_WIDTH - 1) * d), F32)],
        compiler_params=pltpu.CompilerParams(vmem_limit_bytes=_vmem_limit(16 * n_seq * n_new * d * 4)),
        name="short_conv_sample",
    )(wide(bg), wide(cg), wide(u), state.reshape(n_seq, (SC_WIDTH - 1) * d), conv_w)
    return a.reshape(t, d), st.reshape(n_seq, SC_WIDTH - 1, d)


def _post_mix_kernel(*refs, n_in):
    a_refs = refs[:n_in]
    w_refs = refs[n_in:2 * n_in]
    x_ref, gate_ref, g_ref, shift_ref, scale_ref, wr_ref, br_ref, xmid_ref, h_ref, logit_ref = refs[2 * n_in:]
    acc = _dot(a_refs[0][...], w_refs[0][...])
    for a_ref, w_ref in zip(a_refs[1:], w_refs[1:]):
        acc = acc + _dot(a_ref[...], w_ref[...])
    xm = x_ref[...] + _row(gate_ref) * acc
    xmid_ref[...] = xm
    h = _rmsnorm(xm, g_ref[...]) * (1.0 + _row(scale_ref)) + _row(shift_ref)
    h_ref[...] = h
    logit_ref[...] = _dot3(h, wr_ref[...]) + br_ref[...]


def _post_mix(a_list, w_list, x, g, mod, w_route, b_route, name):
    t, d = x.shape
    tm = TOKEN_TILE
    n_in = len(a_list)
    row = lambda w: pl.BlockSpec((tm, w), lambda i: (i, 0))
    const = lambda shp: pl.BlockSpec(shp, lambda i: (0, 0))
    w_bytes = sum(w.size * 2 for w in w_list)
    return pl.pallas_call(
        functools.partial(_post_mix_kernel, n_in=n_in),
        out_shape=[jax.ShapeDtypeStruct((t, d), F32), jax.ShapeDtypeStruct((t, d), F32),
                   jax.ShapeDtypeStruct((t, LANES), F32)],
        grid=(t // tm,),
        in_specs=[row(a.shape[1]) for a in a_list] + [const(w.shape) for w in w_list]
        + [row(d), mod.spec(2, tm), const((1, d)), mod.spec(3, tm), mod.spec(4, tm), const((d, LANES)), const((1, LANES))],
        out_specs=[row(d), row(d), row(LANES)],
        compiler_params=_params(2 * w_bytes + 2 * tm * d * (2 * n_in + 12) + 8 * tm * d * 4, 1),
        name=name,
    )(*a_list, *w_list, x, mod.arr, g.reshape(1, d), mod.arr, mod.arr, w_route, b_route)


def _route_kernel(lg_ref, route_ref, cnt_ref, acc_ref):
    i = pl.program_id(0)
    tm = lg_ref.shape[0]

    @pl.when(i == 0)
    def _():
        acc_ref[...] = jnp.zeros_like(acc_ref)

    lg = lg_ref[...]
    lane = lax.broadcasted_iota(I32, lg.shape, 1)
    neg = -jnp.inf
    gl = jnp.where(lane < N_EXPERT_GROUPS, lg, neg)
    gmax = jnp.max(gl, axis=-1, keepdims=True)
    gidx = jnp.min(jnp.where(gl == gmax, lane, LANES), axis=-1, keepdims=True)
    g_top = 1.0 / jnp.sum(jnp.exp(gl - gmax), axis=-1, keepdims=True)
    lo = ROUTER_LANE0 + gidx * EXPERTS_PER_GROUP
    el = jnp.where((lane >= lo) & (lane < lo + EXPERTS_PER_GROUP), lg, neg)
    m1 = jnp.max(el, axis=-1, keepdims=True)
    i1 = jnp.min(jnp.where(el == m1, lane, LANES), axis=-1, keepdims=True)
    el2 = jnp.where(lane == i1, neg, el)
    m2 = jnp.max(el2, axis=-1, keepdims=True)
    i2 = jnp.min(jnp.where(el2 == m2, lane, LANES), axis=-1, keepdims=True)
    p2 = jnp.exp(m2 - m1)
    w1 = g_top / (1.0 + p2)
    w2 = w1 * p2
    sel1 = lane == i1
    sel2 = lane == i2
    onehot = jnp.where(sel1 | sel2, 1.0, 0.0).astype(BF16)
    r = lax.broadcasted_iota(I32, (tm, tm), 0)
    c = lax.broadcasted_iota(I32, (tm, tm), 1)
    tri = jnp.where(c <= r, 1.0, 0.0).astype(BF16)
    cum = _dot(tri, onehot) + acc_ref[...]
    r1 = jnp.sum(jnp.where(sel1, cum, 0.0), axis=-1, keepdims=True) - 1.0
    r2 = jnp.sum(jnp.where(sel2, cum, 0.0), axis=-1, keepdims=True) - 1.0
    acc_ref[...] = cum[tm - 1:tm, :]
    cnt_ref[...] = cum[tm - 1:tm, :]
    e1 = (i1 - ROUTER_LANE0).astype(F32)
    e2 = (i2 - ROUTER_LANE0).astype(F32)
    rec = jnp.zeros(lg.shape, F32)
    for k, val in enumerate((e1, e2, r1, r2, w1, w2)):
        rec = jnp.where(lane == k, val, rec)
    route_ref[...] = rec


def _route(logits):
    t = logits.shape[0]
    tm = TOKEN_TILE
    return pl.pallas_call(
        _route_kernel,
        out_shape=[jax.ShapeDtypeStruct((t, LANES), F32), jax.ShapeDtypeStruct((1, LANES), F32)],
        grid=(t // tm,),
        in_specs=[pl.BlockSpec((tm, LANES), lambda i: (i, 0))],
        out_specs=[pl.BlockSpec((tm, LANES), lambda i: (i, 0)), pl.BlockSpec((1, LANES), lambda i: (0, 0))],
        scratch_shapes=[pltpu.VMEM((1, LANES), F32)],
        compiler_params=_params(64 * tm * LANES * 4, 1),
        name="moe_route",
    )(logits)


def _row_copy(src_ref, src_row, dst_ref, dst_row, sem):
    return pltpu.make_async_copy(src_ref.at[pl.ds(src_row, 1)], dst_ref.at[pl.ds(dst_row, 1)], sem)


def _dispatch_kernel(dest_ref, h_ref, xs_in_ref, xs_ref, sem):
    del xs_in_ref
    tm = h_ref.shape[0]

    def start(r, _):
        for k in range(TOP_K):
            _row_copy(h_ref, r, xs_ref, dest_ref[TOP_K * r + k], sem).start(priority=k % 2)
        return 0

    lax.fori_loop(0, tm, start, 0, unroll=ROW_DMA_UNROLL)
    for k in range(TOP_K):
        pltpu.make_async_copy(h_ref, xs_ref.at[pl.ds(0, tm)], sem).wait()


def _dispatch(h, dest, xs):
    t, d = h.shape
    tm = min(ROW_DMA_TILE, t)
    return pl.pallas_call(
        _dispatch_kernel,
        out_shape=jax.ShapeDtypeStruct(xs.shape, xs.dtype),
        grid=(t // tm,),
        in_specs=[pl.BlockSpec((TOP_K * tm,), lambda i: (i,), memory_space=pltpu.SMEM),
                  pl.BlockSpec((tm, d), lambda i: (i, 0)),
                  pl.BlockSpec(memory_space=pl.ANY)],
        out_specs=pl.BlockSpec(memory_space=pl.ANY),
        scratch_shapes=[pltpu.SemaphoreType.DMA],
        input_output_aliases={2: 0},
        compiler_params=_params(4 * tm * d * 4, 1),
        name="moe_dispatch",
    )(dest, h, xs)


def _experts_kernel(te_ref, nu_ref, xs_ref, wg_ref, wu_ref, wd_ref, ys_ref):
    i = pl.program_id(0)

    @pl.when(i < nu_ref[0])
    def _():
        x = xs_ref[...].astype(BF16)
        hg = _dot(x, wg_ref[...].astype(BF16))
        hu = _dot(x, wu_ref[...].astype(BF16))
        hid = (_silu(hg) * hu).astype(BF16)
        ys_ref[...] = _dot(hid, wd_ref[...].astype(BF16))

    @pl.when(i >= nu_ref[0])
    def _():
        ys_ref[...] = jnp.zeros_like(ys_ref)


def _experts(xs, tile_expert, n_used, w_gate, w_up, w_down, layer):
    p, d = xs.shape
    tm = MOE_TILE
    f = w_gate.shape[-1]
    row_map = lambda i, te, nu: (jnp.minimum(i, jnp.maximum(nu[0] - 1, 0)), 0)
    return pl.pallas_call(
        _experts_kernel,
        out_shape=jax.ShapeDtypeStruct((p, d), F32),
        grid_spec=pltpu.PrefetchScalarGridSpec(
            num_scalar_prefetch=2,
            grid=(p // tm,),
            in_specs=[pl.BlockSpec((tm, d), row_map),
                      pl.BlockSpec((None, None, d, f), lambda i, te, nu: (layer, te[i], 0, 0)),
                      pl.BlockSpec((None, None, d, f), lambda i, te, nu: (layer, te[i], 0, 0)),
                      pl.BlockSpec((None, None, f, d), lambda i, te, nu: (layer, te[i], 0, 0))],
            out_specs=pl.BlockSpec((tm, d), lambda i, te, nu: (i, 0))),
        compiler_params=_params(2 * 3 * d * f * 4 + 4 * tm * d * 4 + 3 * d * f * 2 + 8 * tm * f * 4, 1),
        name="moe_experts",
    )(tile_expert, n_used, xs, w_gate, w_up, w_down)


def _combine_kernel(*refs, final):
    if final:
        dest_ref, route_ref, x_ref, gate_ref, fg_ref, ys_ref, o_ref, y_ref, gbuf_ref, sem = refs
    else:
        dest_ref, route_ref, x_ref, gate_ref, ys_ref, o_ref, gbuf_ref, sem = refs
    tm = x_ref.shape[0]

    def start(r, _):
        for k in range(TOP_K):
            _row_copy(ys_ref, dest_ref[TOP_K * r + k], gbuf_ref.at[k], r, sem).start(priority=k % 2)
        return 0

    lax.fori_loop(0, tm, start, 0, unroll=ROW_DMA_UNROLL)
    for k in range(TOP_K):
        pltpu.make_async_copy(ys_ref.at[pl.ds(0, tm)], gbuf_ref.at[k], sem).wait()
    route = route_ref[...]
    moe = gbuf_ref[0] * route[:, ROUTE_LANE_W:ROUTE_LANE_W + 1]
    for k in range(1, TOP_K):
        moe = moe + gbuf_ref[k] * route[:, ROUTE_LANE_W + k:ROUTE_LANE_W + k + 1]
    x = x_ref[...] + _row(gate_ref) * moe
    o_ref[...] = x
    if final:
        y_ref[...] = _rmsnorm(x, fg_ref[...])


def _combine(dest, route, x_mid, mod, ys, final_g):
    t, d = x_mid.shape
    tm = min(ROW_DMA_TILE, t)
    final = final_g is not None
    row = lambda w: pl.BlockSpec((tm, w), lambda i: (i, 0))
    in_specs = [pl.BlockSpec((TOP_K * tm,), lambda i: (i,), memory_space=pltpu.SMEM),
                row(LANES), row(d), mod.spec(5, tm)]
    args = [dest, route, x_mid, mod.arr]
    if final:
        in_specs.append(pl.BlockSpec((1, d), lambda i: (0, 0)))
        args.append(final_g.reshape(1, d))
    in_specs.append(pl.BlockSpec(memory_space=pl.ANY))
    args.append(ys)
    out_shape = [jax.ShapeDtypeStruct((t, d), F32)] * (2 if final else 1)
    out_specs = [row(d)] * (2 if final else 1)
    return pl.pallas_call(
        functools.partial(_combine_kernel, final=final),
        out_shape=out_shape,
        grid=(t // tm,),
        in_specs=in_specs,
        out_specs=out_specs,
        scratch_shapes=[pltpu.VMEM((TOP_K, tm, d), F32), pltpu.SemaphoreType.DMA],
        compiler_params=_params(8 * tm * d * 4 + TOP_K * tm * d * 4, 1),
        name="moe_combine",
    )(*args)


def _moe(layer, h_p, h_s, logit_p, logit_s, xmid_p, xmid_s, mod_p, mod_s, w_gate, w_up, w_down, final_g):
    tp, d = h_p.shape
    ts = h_s.shape[0]
    route, counts = _route(jnp.concatenate([logit_p, logit_s], axis=0))
    cnt = counts[0, ROUTER_LANE0:ROUTER_LANE0 + N_EXPERTS].astype(I32)
    padded = (cnt + MOE_TILE - 1) // MOE_TILE * MOE_TILE
    ends = jnp.cumsum(padded)
    starts = ends - padded
    expert = route[:, ROUTE_LANE_E:ROUTE_LANE_E + TOP_K].astype(I32)
    rank = route[:, ROUTE_LANE_R:ROUTE_LANE_R + TOP_K].astype(I32)
    dest = (starts[expert] + rank).reshape(-1)
    n_rows = (tp + ts) * TOP_K + N_EXPERTS * MOE_TILE
    n_tiles = n_rows // MOE_TILE
    n_used = (ends[-1:] // MOE_TILE).astype(I32)
    tile_start = jnp.arange(n_tiles, dtype=I32) * MOE_TILE
    tile_expert = jnp.minimum(jnp.sum((ends[None, :] <= tile_start[:, None]).astype(I32), axis=1), N_EXPERTS - 1)
    xs = jnp.zeros((n_rows, d), F32)
    xs = _dispatch(h_p, dest[:tp * TOP_K], xs)
    xs = _dispatch(h_s, dest[tp * TOP_K:], xs)
    ys = _experts(xs, tile_expert, n_used, w_gate, w_up, w_down, layer)
    out_p = _combine(dest[:tp * TOP_K], route[:tp], xmid_p, mod_p, ys, final_g)
    out_s = _combine(dest[tp * TOP_K:], route[tp:], xmid_s, mod_s, ys, final_g)
    return out_p, out_s


def kernel(x_prompt, x_sample, c_prompt, c_sample, state_ssm, state_ssd_conv, cache_k, cache_v, page_table, state_short_conv, ada_w, ada_b, ln_mix_g, ln_ffn_g, mix_w_in, mix_w_out, ssd_conv_w, ssd_conv_b, ssd_dt_bias, ssd_a_log, ssd_d, ssd_norm_g, sb_bias, sc_w_in, sc_conv_w, sc_w_out, moe_w_group, moe_b_group, moe_w_router, moe_b_router, moe_w_gate, moe_w_up, moe_w_down, final_g):
    b, s, d = x_prompt.shape
    bs, ls, _ = x_sample.shape
    tp, ts = b * s, bs * ls
    depth = ada_w.shape[0]
    xp = x_prompt.reshape(tp, d)
    xs = x_sample.reshape(ts, d)
    mod = _adaln(jnp.concatenate([c_prompt, c_sample], axis=0), ada_w, ada_b)
    ls_pad = SUBLANES

    def pad_rows(a, rows, front=0):
        return jnp.pad(a, ((0, 0), (front, rows - a.shape[1] - front), (0, 0)))

    ssm_p, ssm_s, cv_p, cv_s, k_p, k_s, v_p, v_s, sc_p, sc_s = ([] for _ in range(10))
    y_out = None
    for layer in range(depth):
        i = layer // 2
        mod_p = _Mod(mod[layer, :b].reshape(b, 1, 6 * d), s, d)
        mod_s = _Mod(jnp.repeat(mod[layer, b:], ls, axis=0), ls, d)
        n_route = N_EXPERT_GROUPS + N_EXPERTS
        w_route = jnp.pad(jnp.concatenate([moe_w_group[layer], moe_w_router[layer]], axis=1),
                          ((0, 0), (0, LANES - n_route)))
        b_route = jnp.pad(jnp.concatenate([moe_b_group[layer], moe_b_router[layer]]),
                          (0, LANES - n_route)).reshape(1, LANES)
        if layer % 2 == 0:
            w = mix_w_in[i]
            o_dt = D_SSM + CONV_DIM
            o_q = o_dt + SSM_HEADS
            w_bf = jnp.concatenate(
                [w[:, :o_dt], w[:, o_q:o_q + D_ATT] * SB_SCALE, w[:, o_q + D_ATT:], w[:, o_dt:o_q],
                 jnp.zeros((d, LANES - SSM_HEADS), F32)], axis=1).astype(BF16)
            c0 = D_SSM + CONV_DIM
            splits = ((0, D_SSM, (F32,)), (D_SSM, CONV_DIM, (F32,)), (c0, D_ATT, (BF16,)),
                      (c0 + D_ATT, D_ATT, (F32, BF16)), (c0 + 2 * D_ATT, D_ATT, (F32, BF16)),
                      (c0 + 3 * D_ATT, LANES, (F32,)))
            z1, xbc1, q1, kf1, kb1, vf1, vb1, dt1 = _pre_mix(xp, ln_mix_g[layer], mod_p, w_bf, splits, "pre_mix_even_p")
            z2, xbc2, q2, kf2, kb2, vf2, vb2, dt2 = _pre_mix(xs, ln_mix_g[layer], mod_s, w_bf, splits, "pre_mix_even_s")
            prm = (ssd_conv_w[i], ssd_conv_b[i].reshape(1, CONV_DIM),
                   jnp.pad(ssd_dt_bias[i], (0, LANES - SSM_HEADS)).reshape(1, LANES),
                   jnp.pad(ssd_a_log[i], (0, LANES - SSM_HEADS)).reshape(1, LANES),
                   jnp.repeat(ssd_d[i], SSM_HEAD_DIM).reshape(1, D_SSM),
                   ssd_norm_g[i].reshape(1, D_SSM))
            y1, hfin1, cv1 = _ssd(xbc1, z1, dt1, prm, b, SSD_CHUNK, SSD_CHUNK, name="ssd_prompt")
            seq_pad = lambda a: pad_rows(a.reshape(bs, ls, a.shape[-1]), ls_pad).reshape(bs * ls_pad, a.shape[-1])
            y2, hfin2, cv2 = _ssd(
                seq_pad(xbc2), seq_pad(z2), seq_pad(dt2), prm, bs, ls_pad, ls,
                tail0=pad_rows(state_ssd_conv[i], SUBLANES, front=SUBLANES - (SSD_CONV - 1)),
                h0=state_ssm[i].reshape(bs, D_SSM, D_STATE), name="ssd_sample")
            y2 = y2.reshape(bs, ls_pad, D_SSM)[:, :ls].reshape(ts, D_SSM)
            att1 = _attn_prompt(q1, kb1, vb1, sb_bias[i], b)
            pos_last = lambda a: jnp.transpose(a, (0, 1, 3, 4, 2))
            att2 = _attn_sample(
                q2.reshape(bs, ls, D_ATT), pad_rows(kf2.reshape(bs, ls, D_ATT), SUBLANES),
                pad_rows(vf2.reshape(bs, ls, D_ATT), SUBLANES),
                pos_last(cache_k), pos_last(cache_v), i, page_table, sb_bias[i]).reshape(ts, D_ATT)
            w_out = mix_w_out[i].astype(BF16)
            w_list = [w_out[:D_SSM], w_out[D_SSM:]]
            xm1, hf1, lg1 = _post_mix([y1, att1], w_list, xp, ln_ffn_g[layer], mod_p, w_route, b_route, "post_mix_even_p")
            xm2, hf2, lg2 = _post_mix([y2, att2], w_list, xs, ln_ffn_g[layer], mod_s, w_route, b_route, "post_mix_even_s")
            ssm_p.append(hfin1.reshape(b, SSM_HEADS, SSM_HEAD_DIM, D_STATE))
            ssm_s.append(hfin2.reshape(bs, SSM_HEADS, SSM_HEAD_DIM, D_STATE))
            cv_p.append(cv1)
            cv_s.append(cv2)
            k_p.append(kf1.reshape(b, s, ATT_HEADS, ATT_HEAD_DIM))
            k_s.append(kf2.reshape(bs, ls, ATT_HEADS, ATT_HEAD_DIM))
            v_p.append(vf1.reshape(b, s, ATT_HEADS, ATT_HEAD_DIM))
            v_s.append(vf2.reshape(bs, ls, ATT_HEADS, ATT_HEAD_DIM))
        else:
            w_bf = sc_w_in[i].astype(BF16)
            splits = tuple((j * d, d, (F32,)) for j in range(3))
            bg1, cg1, u1 = _pre_mix(xp, ln_mix_g[layer], mod_p, w_bf, splits, "pre_mix_odd_p")
            bg2, cg2, u2 = _pre_mix(xs, ln_mix_g[layer], mod_s, w_bf, splits, "pre_mix_odd_s")
            a1, st1 = _sc_prompt(bg1, cg1, u1, sc_conv_w[i], b)
            a2, st2 = _sc_sample(bg2, cg2, u2, state_short_conv[i], sc_conv_w[i], bs)
            w_list = [sc_w_out[i].astype(BF16)]
            xm1, hf1, lg1 = _post_mix([a1], w_list, xp, ln_ffn_g[layer], mod_p, w_route, b_route, "post_mix_odd_p")
            xm2, hf2, lg2 = _post_mix([a2], w_list, xs, ln_ffn_g[layer], mod_s, w_route, b_route, "post_mix_odd_s")
            sc_p.append(st1)
            sc_s.append(st2)
        fg = final_g if layer == depth - 1 else None
        out_p, out_s = _moe(layer, hf1, hf2, lg1, lg2, xm1, xm2, mod_p, mod_s,
                            moe_w_gate, moe_w_up, moe_w_down, fg)
        xp, xs = out_p[0], out_s[0]
        if fg is not None:
            y_out = (out_p[1].reshape(b, s, d), out_s[1].reshape(bs, ls, d))
    return (y_out[0], y_out[1], jnp.stack(ssm_p), jnp.stack(ssm_s), jnp.stack(cv_p), jnp.stack(cv_s),
            jnp.stack(k_p), jnp.stack(k_s), jnp.stack(v_p), jnp.stack(v_s), jnp.stack(sc_p), jnp.stack(sc_s))
```

```python
import functools

import jax
import jax.numpy as jnp
from jax import lax
from jax.experimental import pallas as pl
from jax.experimental.pallas import tpu as pltpu

F32 = jnp.float32
BF16 = jnp.bfloat16
I32 = jnp.int32

SSM_HEADS = 16
SSM_HEAD_DIM = 64
D_SSM = SSM_HEADS * SSM_HEAD_DIM
SSM_GROUPS = 2
HEADS_PER_GROUP = SSM_HEADS // SSM_GROUPS
D_STATE = 128
SSD_CONV = 4
SSD_CHUNK = 128
CONV_DIM = D_SSM + 2 * SSM_GROUPS * D_STATE
ATT_HEADS = 16
ATT_HEAD_DIM = 64
D_ATT = ATT_HEADS * ATT_HEAD_DIM
SB_SCALE = ATT_HEAD_DIM ** -0.5
SC_WIDTH = 3
N_EXPERT_GROUPS = 4
EXPERTS_PER_GROUP = 8
N_EXPERTS = N_EXPERT_GROUPS * EXPERTS_PER_GROUP
TOP_K = 2
D_EXPERT = 256
RMS_EPS = 1e-6
NEG_LOG2_E = -1.4426950408889634

LANES = 128
SUBLANES = 8
VMEM_BYTES_V7X = 64 * 1024 * 1024
VMEM_LIMIT_CAP = VMEM_BYTES_V7X - 8 * 1024 * 1024

TOKEN_TILE = 256
ATT_TILE = 256
ATT_HEAD_BLOCK = 4
MOE_TILE = 256
ROW_DMA_TILE = 512
ROW_DMA_UNROLL = 8
ROUTE_LANE_E = 0
ROUTE_LANE_R = 2
ROUTE_LANE_W = 4
ROUTER_LANE0 = N_EXPERT_GROUPS
POSITION_LAST = "f32, position axis last"


def _vmem_limit(nbytes):
    return int(min(max(2 * nbytes, 32 * 1024 * 1024), VMEM_LIMIT_CAP))


def _params(nbytes, n_axes):
    return pltpu.CompilerParams(dimension_semantics=("arbitrary",) * n_axes,
                                vmem_limit_bytes=_vmem_limit(nbytes))


def _dot(a, b):
    return jnp.dot(a, b, preferred_element_type=F32)


def _dot_nt(a, b):
    return lax.dot_general(a, b, (((1,), (1,)), ((), ())), preferred_element_type=F32)


def _dot_tn(a, b):
    return lax.dot_general(a, b, (((0,), (0,)), ((), ())), preferred_element_type=F32)


def _split2(x):
    hi = x.astype(BF16)
    lo = (x - hi.astype(F32)).astype(BF16)
    return hi, lo


def _split3(x):
    hi = x.astype(BF16)
    r = x - hi.astype(F32)
    mid = r.astype(BF16)
    lo = (r - mid.astype(F32)).astype(BF16)
    return hi, mid, lo


def _dot3(a, b):
    ah, al = _split2(a)
    bh, bl = _split2(b)
    return _dot(ah, bh) + (_dot(al, bh) + _dot(ah, bl))


def _dot4(a, b):
    m, n = a.shape[0], b.shape[1]
    r = _dot(jnp.concatenate(_split2(a), axis=0), jnp.concatenate(_split2(b), axis=1))
    return (r[:m, :n] + r[m:, n:]) + (r[:m, n:] + r[m:, :n])


def _silu(x):
    return x / (1.0 + jnp.exp(-x))


def _softplus(x):
    return jnp.maximum(x, 0.0) + jnp.log(1.0 + jnp.exp2(jnp.abs(x) * NEG_LOG2_E))


def _rmsnorm(x, g):
    return x * lax.rsqrt(jnp.mean(x * x, axis=-1, keepdims=True) + RMS_EPS) * g


def _row(ref):
    v = ref[...]
    return v[0] if v.ndim == 3 else v


def _adaln_kernel(c_ref, w_ref, b_ref, o_ref):
    o_ref[...] = _dot3(_silu(c_ref[...]), w_ref[...]) + b_ref[...]


def _adaln(c_all, ada_w, ada_b):
    depth, d, n = ada_w.shape
    rows = c_all.shape[0]
    tn = 1024
    return pl.pallas_call(
        _adaln_kernel,
        out_shape=jax.ShapeDtypeStruct((depth, rows, n), F32),
        grid=(depth, n // tn),
        in_specs=[pl.BlockSpec((rows, d), lambda l, j: (0, 0)),
                  pl.BlockSpec((None, d, tn), lambda l, j: (l, 0, j)),
                  pl.BlockSpec((None, 1, tn), lambda l, j: (l, 0, j))],
        out_specs=pl.BlockSpec((None, rows, tn), lambda l, j: (l, 0, j)),
        compiler_params=_params(2 * d * tn * 4 + 4 * rows * (d + tn) * 4, 2),
        name="adaln",
    )(c_all, ada_w, ada_b.reshape(depth, 1, n))


class _Mod:
    def __init__(self, arr, seq_len, d):
        self.arr = arr
        self.seq_len = seq_len
        self.d = d

    def spec(self, which, tm):
        d = self.d
        if self.arr.ndim == 3:
            tiles_per_seq = self.seq_len // tm
            return pl.BlockSpec((1, 1, d), lambda i, *_: (i // tiles_per_seq, 0, which))
        return pl.BlockSpec((tm, d), lambda i, *_: (i, which))


def _pre_mix_kernel(x_ref, g_ref, shift_ref, scale_ref, w_ref, *out_refs, splits):
    h = (_rmsnorm(x_ref[...], g_ref[...]) * (1.0 + _row(scale_ref)) + _row(shift_ref)).astype(BF16)
    k = 0
    for start, width, dtypes in splits:
        acc = _dot(h, w_ref[:, start:start + width])
        for dt in dtypes:
            if dt == POSITION_LAST:
                out_refs[k][...] = acc.T
            else:
                out_refs[k][...] = acc.astype(dt)
            k += 1


def _pre_mix(x, g, mod, w_bf, splits, name):
    t, d = x.shape
    n = w_bf.shape[1]
    tm = TOKEN_TILE
    tiles_per_seq = mod.seq_len // tm
    out_shape, out_specs, out_bytes = [], [], 0
    for _, width, dtypes in splits:
        for dt in dtypes:
            if dt == POSITION_LAST:
                out_shape.append(jax.ShapeDtypeStruct((t // mod.seq_len, width, mod.seq_len), F32))
                out_specs.append(pl.BlockSpec((None, width, tm), lambda i: (i // tiles_per_seq, 0, i % tiles_per_seq)))
                out_bytes += tm * width * 4
            else:
                out_shape.append(jax.ShapeDtypeStruct((t, width), dt))
                out_specs.append(pl.BlockSpec((tm, width), lambda i: (i, 0)))
                out_bytes += tm * width * jnp.dtype(dt).itemsize
    return pl.pallas_call(
        functools.partial(_pre_mix_kernel, splits=splits),
        out_shape=out_shape,
        grid=(t // tm,),
        in_specs=[pl.BlockSpec((tm, d), lambda i: (i, 0)),
                  pl.BlockSpec((1, d), lambda i: (0, 0)),
                  mod.spec(0, tm), mod.spec(1, tm),
                  pl.BlockSpec((d, n), lambda i: (0, 0))],
        out_specs=out_specs,
        compiler_params=_params(2 * d * n * 2 + 2 * tm * d * 4 + 2 * out_bytes + tm * n * 4, 1),
        name=name,
    )(x, g.reshape(1, d), mod.arr, mod.arr, w_bf)


def _ssd_kernel(*refs, chunk, n_real, has_init):
    if has_init:
        (xbc_ref, z_ref, dt_ref, tail0_ref, h0_ref, cw_ref, cb_ref, dtb_ref, alog_ref, dsk_ref, ng_ref,
         y_ref, hout_ref, cvout_ref, buf_ref, state_ref, ybuf_ref) = refs
    else:
        (xbc_ref, z_ref, dt_ref, cw_ref, cb_ref, dtb_ref, alog_ref, dsk_ref, ng_ref,
         y_ref, hout_ref, cvout_ref, buf_ref, state_ref, ybuf_ref) = refs
    c = pl.program_id(1)
    last = pl.num_programs(1) - 1
    L = chunk
    T0 = SUBLANES

    @pl.when(c == 0)
    def _():
        if has_init:
            buf_ref[0:T0, :] = tail0_ref[0]
            state_ref[...] = h0_ref[0]
        else:
            buf_ref[0:T0, :] = jnp.zeros((T0, CONV_DIM), F32)
            state_ref[...] = jnp.zeros_like(state_ref)

    xbc = xbc_ref[...]
    buf_ref[T0:T0 + L, :] = xbc
    x_ext = buf_ref[0:T0 + L, :]
    conv = xbc * cw_ref[SSD_CONV - 1:SSD_CONV, :]
    for j in range(SSD_CONV - 1):
        shifted = pltpu.roll(x_ext, SSD_CONV - 1 - j, axis=0)[T0:]
        conv = conv + shifted * cw_ref[j:j + 1, :]
    xc = _silu(conv + cb_ref[...])
    xs = xc[:, :D_SSM]
    bmat = xc[:, D_SSM:D_SSM + SSM_GROUPS * D_STATE].astype(BF16)
    cmat = xc[:, D_SSM + SSM_GROUPS * D_STATE:].astype(BF16)

    @pl.when(c == last)
    def _():
        cvout_ref[0] = buf_ref[pl.ds(T0 + n_real - (SSD_CONV - 1), SSD_CONV - 1), :]

    if n_real == L:
        buf_ref[0:T0, :] = buf_ref[L:L + T0, :]

    dt = _softplus(dt_ref[...] + dtb_ref[...])
    rows = lax.broadcasted_iota(I32, (L, L), 0)
    cols = lax.broadcasted_iota(I32, (L, L), 1)
    if n_real < L:
        dt = jnp.where(lax.broadcasted_iota(I32, dt.shape, 0) < n_real, dt, 0.0)
    a = -jnp.exp(alog_ref[...])
    tri = cols <= rows
    tri_bf = jnp.where(tri, 1.0, 0.0).astype(BF16)
    da_hi, da_mid, da_lo = _split3(dt * a)
    acum = _dot(tri_bf, da_hi) + (_dot(tri_bf, da_mid) + _dot(tri_bf, da_lo))
    acum_t = acum.T
    a_last = acum[L - 1:L, :]
    chunk_decay = jnp.exp(a_last)
    per_head = jnp.concatenate([dt, jnp.exp(acum), jnp.exp(a_last - acum)], axis=0)
    head_of_lane = lax.broadcasted_iota(I32, (3 * LANES, D_SSM), 1) // SSM_HEAD_DIM
    expand3 = jnp.where(lax.broadcasted_iota(I32, (3 * LANES, D_SSM), 0) % LANES == head_of_lane, 1.0, 0.0).astype(BF16)
    wide = _dot(jnp.concatenate(_split3(per_head), axis=1), expand3)
    dt_w, e_acum_w, to_end_w = wide[:L], wide[L:2 * L], wide[2 * L:]
    xdt = xs * dt_w
    xdt_bf = xdt.astype(BF16)
    x_end = (xdt * to_end_w).astype(BF16)

    gw = HEADS_PER_GROUP * SSM_HEAD_DIM
    for g in range(SSM_GROUPS):
        gl = slice(g * gw, (g + 1) * gw)
        gs = slice(g * D_STATE, (g + 1) * D_STATE)
        st = state_ref[gl, :]
        ybuf_ref[:, gl] = _dot_nt(cmat[:, gs], st.astype(BF16)) * e_acum_w[:, gl] + xs[:, gl] * dsk_ref[:, gl]
        s_new = _dot_tn(x_end[:, gl], bmat[:, gs])
        for h in range(g * HEADS_PER_GROUP, (g + 1) * HEADS_PER_GROUP):
            hs = slice(h * SSM_HEAD_DIM, (h + 1) * SSM_HEAD_DIM)
            hg = slice(hs.start - gl.start, hs.stop - gl.start)
            state_ref[hs, :] = st[hg] * chunk_decay[:, h:h + 1] + s_new[hg]

    cb = [_dot_nt(cmat[:, g * D_STATE:(g + 1) * D_STATE], bmat[:, g * D_STATE:(g + 1) * D_STATE])
          for g in range(SSM_GROUPS)]
    first_half = lax.broadcasted_iota(I32, (1, LANES), 1) < SSM_HEAD_DIM
    for pair in range(SSM_HEADS // 2):
        g = (2 * pair) // HEADS_PER_GROUP
        pl_ = slice(pair * LANES, (pair + 1) * LANES)
        ms = []
        for h in (2 * pair, 2 * pair + 1):
            seg = acum[:, h:h + 1] - acum_t[h:h + 1, :]
            ms.append((cb[g] * jnp.exp(jnp.where(tri, seg, -jnp.inf))).astype(BF16))
        xp = xdt_bf[:, pl_]
        zero = jnp.zeros_like(xp)
        x_bd = jnp.concatenate([jnp.where(first_half, xp, zero), jnp.where(first_half, zero, xp)], axis=0)
        ybuf_ref[:, pl_] += _dot(jnp.concatenate(ms, axis=1), x_bd)

    y = ybuf_ref[...] * _silu(z_ref[...].astype(F32))
    half = D_SSM // SSM_GROUPS
    for g in range(SSM_GROUPS):
        yg = y[:, g * half:(g + 1) * half]
        y_ref[:, g * half:(g + 1) * half] = (
            yg * lax.rsqrt(jnp.mean(yg * yg, axis=-1, keepdims=True) + RMS_EPS) * ng_ref[:, g * half:(g + 1) * half]
        ).astype(y_ref.dtype)

    @pl.when(c == last)
    def _():
        hout_ref[0] = state_ref[...]


def _ssd(xbc, z, dt, prm, n_seq, chunk, n_real, tail0=None, h0=None, name="ssd"):
    t = xbc.shape[0]
    nc = t // (n_seq * chunk)
    has_init = tail0 is not None
    cw, cb, dtb, alog, dsk, ng = prm
    row_spec = lambda w: pl.BlockSpec((chunk, w), lambda b, c: (b * nc + c, 0))
    const = lambda shp: pl.BlockSpec(shp, lambda b, c: (0,) * len(shp))
    in_specs = [row_spec(CONV_DIM), row_spec(D_SSM), row_spec(LANES)]
    args = [xbc, z, dt]
    if has_init:
        in_specs += [pl.BlockSpec((1, SUBLANES, CONV_DIM), lambda b, c: (b, 0, 0)),
                     pl.BlockSpec((1, D_SSM, D_STATE), lambda b, c: (b, 0, 0))]
        args += [tail0, h0]
    in_specs += [const((SSD_CONV, CONV_DIM)), const((1, CONV_DIM)), const((1, LANES)), const((1, LANES)),
                 const((1, D_SSM)), const((1, D_SSM))]
    args += [cw, cb, dtb, alog, dsk, ng]
    scratch_bytes = ((chunk + 2 * SUBLANES) * CONV_DIM + D_SSM * D_STATE + chunk * D_SSM) * 4
    block_bytes = chunk * (CONV_DIM + D_SSM + LANES) * 4 + chunk * D_SSM * 2 + 2 * D_SSM * D_STATE * 4
    return pl.pallas_call(
        functools.partial(_ssd_kernel, chunk=chunk, n_real=n_real, has_init=has_init),
        out_shape=[jax.ShapeDtypeStruct((t, D_SSM), BF16),
                   jax.ShapeDtypeStruct((n_seq, D_SSM, D_STATE), F32),
                   jax.ShapeDtypeStruct((n_seq, SSD_CONV - 1, CONV_DIM), F32)],
        grid=(n_seq, nc),
        in_specs=in_specs,
        out_specs=[row_spec(D_SSM),
                   pl.BlockSpec((1, D_SSM, D_STATE), lambda b, c: (b, 0, 0)),
                   pl.BlockSpec((1, SSD_CONV - 1, CONV_DIM), lambda b, c: (b, 0, 0))],
        scratch_shapes=[pltpu.VMEM((chunk + 2 * SUBLANES, CONV_DIM), F32),
                        pltpu.VMEM((D_SSM, D_STATE), F32),
                        pltpu.VMEM((chunk, D_SSM), F32)],
        compiler_params=_params(2 * block_bytes + scratch_bytes + 8 * chunk * CONV_DIM * 4, 2),
        name=name,
    )(*args)


MASKED_LOGIT = -1e30


def _sb_scores(s, neg_upper2):
    sp = _softplus(s)
    hi, lo = _split2(sp)
    later = _dot(jnp.concatenate([hi, lo], axis=1), neg_upper2)
    return s - sp, later, later[:, 0:1] - sp[:, 0:1]


def _neg_strict_upper2(n):
    j = lax.broadcasted_iota(I32, (2 * n, n), 0) % n
    k = lax.broadcasted_iota(I32, (2 * n, n), 1)
    return jnp.where(j > k, -1.0, 0.0).astype(BF16)


def _attn_prompt_kernel(bias_ref, q_ref, k_ref, v_ref, o_ref, qs_ref, carry_ref, acc_ref):
    hq = pl.program_id(1)
    qi = pl.program_id(2)
    tq, width = q_ref.shape
    tk = ATT_TILE
    nh = ATT_HEAD_BLOCK
    m = nh * tq
    lane_head = lax.broadcasted_iota(I32, (1, width), 1) // ATT_HEAD_DIM
    q = q_ref[...]
    for h in range(nh):
        qs_ref[h * tq:(h + 1) * tq, :] = jnp.where(lane_head == h, q, jnp.zeros_like(q))
    row_head = lax.broadcasted_iota(I32, (m, 1), 0) // tq
    bias = jnp.zeros((m, 1), F32)
    for h in range(nh):
        bias = jnp.where(row_head == h, bias_ref[hq * nh + h], bias)
    neg_upper2 = _neg_strict_upper2(tk)
    causal = lax.broadcasted_iota(I32, (m, tk), 1) < lax.broadcasted_iota(I32, (m, tk), 0) % tq

    def tile(j, mask):
        start = pl.multiple_of(j * tk, tk)
        vt = v_ref[pl.ds(start, tk), :]
        s = _dot_nt(qs_ref[...], k_ref[pl.ds(start, tk), :]) + bias
        if mask is not None:
            s = jnp.where(mask, s, MASKED_LOGIT)
        log_beta, later, total = _sb_scores(s, neg_upper2)
        carry = carry_ref[...]
        w = jnp.exp(log_beta + (later + carry)).astype(BF16)
        carry_ref[...] = carry + total
        w_cat = jnp.concatenate([w[h * tq:(h + 1) * tq] for h in range(nh)], axis=1)
        v_cat = jnp.concatenate([jnp.where(lane_head == h, vt, jnp.zeros_like(vt)) for h in range(nh)], axis=0)
        acc_ref[...] += _dot(w_cat, v_cat)

    carry_ref[...] = jnp.zeros_like(carry_ref)
    acc_ref[...] = jnp.zeros_like(acc_ref)
    tile(qi, causal)

    def body(i, c):
        tile(qi - 1 - i, None)
        return c

    lax.fori_loop(0, qi, body, 0)
    o_ref[...] = acc_ref[...].astype(o_ref.dtype)


def _attn_prompt(q, k, v, bias, n_seq):
    t, d = q.shape
    s = t // n_seq
    tq = ATT_TILE
    nq = s // tq
    width = ATT_HEAD_BLOCK * ATT_HEAD_DIM
    return pl.pallas_call(
        _attn_prompt_kernel,
        out_shape=jax.ShapeDtypeStruct((t, d), BF16),
        grid_spec=pltpu.PrefetchScalarGridSpec(
            num_scalar_prefetch=1,
            grid=(n_seq, d // width, nq),
            in_specs=[pl.BlockSpec((tq, width), lambda b, hq, qi, *_: (b * nq + qi, hq)),
                      pl.BlockSpec((s, width), lambda b, hq, qi, *_: (b, hq)),
                      pl.BlockSpec((s, width), lambda b, hq, qi, *_: (b, hq))],
            out_specs=pl.BlockSpec((tq, width), lambda b, hq, qi, *_: (b * nq + qi, hq)),
            scratch_shapes=[pltpu.VMEM((ATT_HEAD_BLOCK * tq, width), BF16),
                            pltpu.VMEM((ATT_HEAD_BLOCK * tq, 1), F32),
                            pltpu.VMEM((tq, width), F32)]),
        compiler_params=_params(4 * s * width * 2 + 4 * tq * width * 2
                                + 12 * ATT_HEAD_BLOCK * tq * ATT_TILE * 4, 3),
        name="sb_attn_prompt",
    )(bias, q, k, v)


def _attn_sample_kernel(pt_ref, bias_ref, q_ref, kn_ref, vn_ref, *refs, n_pages, n_new):
    k_refs = refs[:n_pages]
    v_refs = refs[n_pages:2 * n_pages]
    o_ref = refs[2 * n_pages]
    d = q_ref.shape[-1]
    page = k_refs[0].shape[-1]
    n_rows = n_new * ATT_HEADS
    n_blocks = n_pages + 1
    m = n_blocks * n_rows
    row_head = lax.broadcasted_iota(I32, (n_rows, 1), 0) % ATT_HEADS
    lane_head = lax.broadcasted_iota(I32, (1, d), 1) // ATT_HEAD_DIM
    own = lane_head == row_head
    q = q_ref[0]
    qx = jnp.broadcast_to(q[:, None, :], (n_new, ATT_HEADS, d)).reshape(n_rows, d)
    qx = jnp.where(own, qx, jnp.zeros_like(qx))
    bias = jnp.zeros((n_rows, 1), F32)
    for h in range(ATT_HEADS):
        bias = jnp.where(row_head == h, bias_ref[h], bias)

    pad = jnp.zeros((page - kn_ref.shape[1], d), F32)
    k_new = jnp.concatenate([kn_ref[0], pad], axis=0).astype(BF16)
    v_new = jnp.concatenate([vn_ref[0], pad], axis=0).astype(BF16)
    order = range(n_pages - 1, -1, -1)
    kt = jnp.concatenate([k_refs[p][...].reshape(d, page).astype(BF16) for p in order], axis=1)
    vt = jnp.concatenate([v_refs[p][...].reshape(d, page).astype(BF16) for p in order], axis=1)

    s_pages = _dot(qx, kt)
    s = jnp.concatenate([_dot_nt(qx, k_new)] + [s_pages[:, i * page:(i + 1) * page] for i in range(n_pages)], axis=0)
    s = s + jnp.concatenate([bias] * n_blocks, axis=0)
    row = lax.broadcasted_iota(I32, (m, page), 0)
    col = lax.broadcasted_iota(I32, (m, page), 1)
    s = jnp.where((row >= n_rows) | (col < row // ATT_HEADS), s, MASKED_LOGIT)
    log_beta, later, total = _sb_scores(s, _neg_strict_upper2(page))
    carry = jnp.zeros((n_rows, 1), F32)
    carries = []
    for i in range(n_blocks):
        carries.append(carry)
        carry = carry + total[i * n_rows:(i + 1) * n_rows]
    w = jnp.exp(log_beta + (later + jnp.concatenate(carries, axis=0))).astype(BF16)
    w_pages = jnp.concatenate([w[(i + 1) * n_rows:(i + 2) * n_rows] for i in range(n_pages)], axis=1)
    acc = _dot(w[:n_rows], v_new) + _dot_nt(w_pages, vt)
    acc = jnp.where(own, acc, 0.0)
    o_ref[0] = jnp.sum(acc.reshape(n_new, ATT_HEADS, d), axis=1).astype(o_ref.dtype)


def _attn_sample(q, k_new, v_new, cache_kt, cache_vt, layer, page_table, bias):
    bs, n_new, d = q.shape
    n_pages = page_table.shape[1]
    _, _, n_heads, dh, page = cache_kt.shape
    page_spec = lambda p: pl.BlockSpec((None, None, n_heads, dh, page),
                                       lambda b, pt, bias_: (layer, pt[b, p], 0, 0, 0))
    seq_spec = lambda rows: pl.BlockSpec((1, rows, d), lambda b, pt, bias_: (b, 0, 0))
    page_bytes = page * d * 4
    return pl.pallas_call(
        functools.partial(_attn_sample_kernel, n_pages=n_pages, n_new=n_new),
        out_shape=jax.ShapeDtypeStruct((bs, n_new, d), BF16),
        grid_spec=pltpu.PrefetchScalarGridSpec(
            num_scalar_prefetch=2,
            grid=(bs,),
            in_specs=[seq_spec(n_new), seq_spec(k_new.shape[1]), seq_spec(v_new.shape[1])]
            + [page_spec(p) for p in range(n_pages)] * 2,
            out_specs=seq_spec(n_new)),
        compiler_params=_params(2 * 2 * n_pages * page_bytes + 8 * page_bytes, 1),
        name="sb_attn_sample",
    )(page_table, bias, q, k_new, v_new, *([cache_kt] * n_pages), *([cache_vt] * n_pages))


def _sc_prompt_kernel(b_ref, c_ref, u_ref, w_ref, a_ref, st_ref, buf_ref):
    L = c_ref.shape[0]
    T0 = SUBLANES
    cu = c_ref[...].astype(F32) * u_ref[...].astype(F32)
    buf_ref[0:T0, :] = jnp.zeros((T0, cu.shape[1]), F32)
    buf_ref[T0:T0 + L, :] = cu
    conv = cu * w_ref[SC_WIDTH - 1:SC_WIDTH, :]
    for j in range(SC_WIDTH - 1):
        conv = conv + buf_ref[pl.ds(T0 - (SC_WIDTH - 1) + j, L), :] * w_ref[j:j + 1, :]
    a_ref[...] = (b_ref[...].astype(F32) * conv).astype(a_ref.dtype)
    st_ref[0] = buf_ref[pl.ds(T0 + L - (SC_WIDTH - 1), SC_WIDTH - 1), :]


def _sc_prompt(bg, cg, u, conv_w, n_seq):
    t, d = cg.shape
    s = t // n_seq
    wl = 256
    spec = pl.BlockSpec((s, wl), lambda b, j: (b, j))
    return pl.pallas_call(
        _sc_prompt_kernel,
        out_shape=[jax.ShapeDtypeStruct((t, d), BF16),
                   jax.ShapeDtypeStruct((n_seq, SC_WIDTH - 1, d), F32)],
        grid=(n_seq, d // wl),
        in_specs=[spec, spec, spec, pl.BlockSpec((SC_WIDTH, wl), lambda b, j: (0, j))],
        out_specs=[spec, pl.BlockSpec((1, SC_WIDTH - 1, wl), lambda b, j: (b, 0, j))],
        scratch_shapes=[pltpu.VMEM((s + SUBLANES, wl), F32)],
        compiler_params=_params(2 * 3 * s * wl * 4 + 2 * s * wl * 2 + 6 * s * wl * 4, 2),
        name="short_conv_prompt",
    )(bg, cg, u, conv_w)


def _sc_sample_kernel(b_ref, c_ref, u_ref, st_ref, w_ref, a_ref, sto_ref, *, n_new, d):
    up = [st_ref[:, j * d:(j + 1) * d] for j in range(SC_WIDTH - 1)]
    up += [c_ref[:, t * d:(t + 1) * d].astype(F32) * u_ref[:, t * d:(t + 1) * d].astype(F32) for t in range(n_new)]
    for t in range(n_new):
        conv = up[t] * w_ref[0:1, :]
        for j in range(1, SC_WIDTH):
            conv = conv + up[t + j] * w_ref[j:j + 1, :]
        a_ref[:, t * d:(t + 1) * d] = (b_ref[:, t * d:(t + 1) * d].astype(F32) * conv).astype(a_ref.dtype)
    for j in range(SC_WIDTH - 1):
        sto_ref[:, j * d:(j + 1) * d] = up[n_new + j]


def _sc_sample(bg, cg, u, state, conv_w, n_seq):
    t, d = cg.shape
    n_new = t // n_seq
    wide = lambda x: x.reshape(n_seq, n_new * d)
    a, st = pl.pallas_call(
        functools.partial(_sc_sample_kernel, n_new=n_new, d=d),
        out_shape=[jax.ShapeDtypeStruct((n_seq, n_new * d), BF16),
                   jax.ShapeDtypeStruct((n_seq, (SC_WIDTH - 1) * d), F32)],
        compiler_params=pltpu.CompilerParams(vmem_limit_bytes=_vmem_limit(16 * n_seq * n_new * d * 4)),
        name="short_conv_sample",
    )(wide(bg), wide(cg), wide(u), state.reshape(n_seq, (SC_WIDTH - 1) * d), conv_w)
    return a.reshape(t, d), st.reshape(n_seq, SC_WIDTH - 1, d)


def _post_mix_kernel(*refs, n_in):
    a_refs = refs[:n_in]
    w_refs = refs[n_in:2 * n_in]
    x_ref, gate_ref, g_ref, shift_ref, scale_ref, wr_ref, br_ref, xmid_ref, h_ref, logit_ref = refs[2 * n_in:]
    acc = _dot(a_refs[0][...], w_refs[0][...])
    for a_ref, w_ref in zip(a_refs[1:], w_refs[1:]):
        acc = acc + _dot(a_ref[...], w_ref[...])
    xm = x_ref[...] + _row(gate_ref) * acc
    xmid_ref[...] = xm
    h = _rmsnorm(xm, g_ref[...]) * (1.0 + _row(scale_ref)) + _row(shift_ref)
    h_ref[...] = h
    logit_ref[...] = _dot4(h, wr_ref[...]) + br_ref[...]


def _post_mix(a_list, w_list, x, g, mod, w_route, b_route, name):
    t, d = x.shape
    tm = TOKEN_TILE
    n_in = len(a_list)
    row = lambda w: pl.BlockSpec((tm, w), lambda i: (i, 0))
    const = lambda shp: pl.BlockSpec(shp, lambda i: (0, 0))
    w_bytes = sum(w.size * 2 for w in w_list)
    return pl.pallas_call(
        functools.partial(_post_mix_kernel, n_in=n_in),
        out_shape=[jax.ShapeDtypeStruct((t, d), F32), jax.ShapeDtypeStruct((t, d), F32),
                   jax.ShapeDtypeStruct((t, LANES), F32)],
        grid=(t // tm,),
        in_specs=[row(a.shape[1]) for a in a_list] + [const(w.shape) for w in w_list]
        + [row(d), mod.spec(2, tm), const((1, d)), mod.spec(3, tm), mod.spec(4, tm), const((d, LANES)), const((1, LANES))],
        out_specs=[row(d), row(d), row(LANES)],
        compiler_params=_params(2 * w_bytes + 2 * tm * d * (2 * n_in + 12) + 8 * tm * d * 4, 1),
        name=name,
    )(*a_list, *w_list, x, mod.arr, g.reshape(1, d), mod.arr, mod.arr, w_route, b_route)


def _route_kernel(lg_ref, route_ref, cnt_ref, acc_ref):
    i = pl.program_id(0)
    tm = lg_ref.shape[0]

    @pl.when(i == 0)
    def _():
        acc_ref[...] = jnp.zeros_like(acc_ref)

    lg = lg_ref[...]
    lane = lax.broadcasted_iota(I32, lg.shape, 1)
    neg = -jnp.inf
    gl = jnp.where(lane < N_EXPERT_GROUPS, lg, neg)
    gmax = jnp.max(gl, axis=-1, keepdims=True)
    gidx = jnp.min(jnp.where(gl == gmax, lane, LANES), axis=-1, keepdims=True)
    g_top = 1.0 / jnp.sum(jnp.exp(gl - gmax), axis=-1, keepdims=True)
    lo = ROUTER_LANE0 + gidx * EXPERTS_PER_GROUP
    el = jnp.where((lane >= lo) & (lane < lo + EXPERTS_PER_GROUP), lg, neg)
    m1 = jnp.max(el, axis=-1, keepdims=True)
    i1 = jnp.min(jnp.where(el == m1, lane, LANES), axis=-1, keepdims=True)
    el2 = jnp.where(lane == i1, neg, el)
    m2 = jnp.max(el2, axis=-1, keepdims=True)
    i2 = jnp.min(jnp.where(el2 == m2, lane, LANES), axis=-1, keepdims=True)
    p2 = jnp.exp(m2 - m1)
    w1 = g_top / (1.0 + p2)
    w2 = w1 * p2
    sel1 = lane == i1
    sel2 = lane == i2
    onehot = jnp.where(sel1 | sel2, 1.0, 0.0).astype(BF16)
    r = lax.broadcasted_iota(I32, (tm, tm), 0)
    c = lax.broadcasted_iota(I32, (tm, tm), 1)
    tri = jnp.where(c <= r, 1.0, 0.0).astype(BF16)
    cum = _dot(tri, onehot) + acc_ref[...]
    r1 = jnp.sum(jnp.where(sel1, cum, 0.0), axis=-1, keepdims=True) - 1.0
    r2 = jnp.sum(jnp.where(sel2, cum, 0.0), axis=-1, keepdims=True) - 1.0
    acc_ref[...] = cum[tm - 1:tm, :]
    cnt_ref[...] = cum[tm - 1:tm, :]
    e1 = (i1 - ROUTER_LANE0).astype(F32)
    e2 = (i2 - ROUTER_LANE0).astype(F32)
    rec = jnp.zeros(lg.shape, F32)
    for k, val in enumerate((e1, e2, r1, r2, w1, w2)):
        rec = jnp.where(lane == k, val, rec)
    route_ref[...] = rec


def _route(logits):
    t = logits.shape[0]
    tm = TOKEN_TILE
    return pl.pallas_call(
        _route_kernel,
        out_shape=[jax.ShapeDtypeStruct((t, LANES), F32), jax.ShapeDtypeStruct((1, LANES), F32)],
        grid=(t // tm,),
        in_specs=[pl.BlockSpec((tm, LANES), lambda i: (i, 0))],
        out_specs=[pl.BlockSpec((tm, LANES), lambda i: (i, 0)), pl.BlockSpec((1, LANES), lambda i: (0, 0))],
        scratch_shapes=[pltpu.VMEM((1, LANES), F32)],
        compiler_params=_params(64 * tm * LANES * 4, 1),
        name="moe_route",
    )(logits)


def _row_copy(src_ref, src_row, dst_ref, dst_row, sem):
    return pltpu.make_async_copy(src_ref.at[pl.ds(src_row, 1)], dst_ref.at[pl.ds(dst_row, 1)], sem)


def _dispatch_kernel(dest_ref, h_ref, xs_in_ref, xs_ref, sem):
    del xs_in_ref
    tm = h_ref.shape[0]

    def start(r, _):
        for k in range(TOP_K):
            _row_copy(h_ref, r, xs_ref, dest_ref[k * tm + r], sem).start(priority=k % 2)
        return 0

    lax.fori_loop(0, tm, start, 0, unroll=ROW_DMA_UNROLL)
    for k in range(TOP_K):
        pltpu.make_async_copy(h_ref, xs_ref.at[pl.ds(0, tm)], sem).wait()


def _dispatch(h, dest, xs):
    t, d = h.shape
    tm = min(ROW_DMA_TILE, t)
    return pl.pallas_call(
        _dispatch_kernel,
        out_shape=jax.ShapeDtypeStruct(xs.shape, xs.dtype),
        grid=(t // tm,),
        in_specs=[pl.BlockSpec((TOP_K * tm,), lambda i: (i,), memory_space=pltpu.SMEM),
                  pl.BlockSpec((tm, d), lambda i: (i, 0)),
                  pl.BlockSpec(memory_space=pl.ANY)],
        out_specs=pl.BlockSpec(memory_space=pl.ANY),
        scratch_shapes=[pltpu.SemaphoreType.DMA],
        input_output_aliases={2: 0},
        compiler_params=_params(4 * tm * d * 4, 1),
        name="moe_dispatch",
    )(dest, h, xs)


def _experts_kernel(te_ref, nu_ref, xs_ref, wg_ref, wu_ref, wd_ref, ys_ref):
    i = pl.program_id(0)

    @pl.when(i < nu_ref[0])
    def _():
        x = xs_ref[...].astype(BF16)
        hg = _dot(x, wg_ref[...].astype(BF16))
        hu = _dot(x, wu_ref[...].astype(BF16))
        hid = (_silu(hg) * hu).astype(BF16)
        ys_ref[...] = _dot(hid, wd_ref[...].astype(BF16))

    @pl.when(i >= nu_ref[0])
    def _():
        ys_ref[...] = jnp.zeros_like(ys_ref)


def _experts(xs, tile_expert, n_used, w_gate, w_up, w_down, layer):
    p, d = xs.shape
    tm = MOE_TILE
    f = w_gate.shape[-1]
    row_map = lambda i, te, nu: (jnp.minimum(i, jnp.maximum(nu[0] - 1, 0)), 0)
    return pl.pallas_call(
        _experts_kernel,
        out_shape=jax.ShapeDtypeStruct((p, d), F32),
        grid_spec=pltpu.PrefetchScalarGridSpec(
            num_scalar_prefetch=2,
            grid=(p // tm,),
            in_specs=[pl.BlockSpec((tm, d), row_map),
                      pl.BlockSpec((None, None, d, f), lambda i, te, nu: (layer, te[i], 0, 0)),
                      pl.BlockSpec((None, None, d, f), lambda i, te, nu: (layer, te[i], 0, 0)),
                      pl.BlockSpec((None, None, f, d), lambda i, te, nu: (layer, te[i], 0, 0))],
            out_specs=pl.BlockSpec((tm, d), lambda i, te, nu: (i, 0))),
        compiler_params=_params(2 * 3 * d * f * 4 + 4 * tm * d * 4 + 3 * d * f * 2 + 8 * tm * f * 4, 1),
        name="moe_experts",
    )(tile_expert, n_used, xs, w_gate, w_up, w_down)


def _combine_kernel(*refs, final):
    if final:
        dest_ref, route_ref, x_ref, gate_ref, fg_ref, ys_ref, o_ref, y_ref, gbuf_ref, sem = refs
    else:
        dest_ref, route_ref, x_ref, gate_ref, ys_ref, o_ref, gbuf_ref, sem = refs
    tm = x_ref.shape[0]

    def start(r, _):
        for k in range(TOP_K):
            _row_copy(ys_ref, dest_ref[k * tm + r], gbuf_ref.at[k], r, sem).start(priority=k % 2)
        return 0

    lax.fori_loop(0, tm, start, 0, unroll=ROW_DMA_UNROLL)
    for k in range(TOP_K):
        pltpu.make_async_copy(ys_ref.at[pl.ds(0, tm)], gbuf_ref.at[k], sem).wait()
    route = route_ref[...]
    moe = gbuf_ref[0] * route[:, ROUTE_LANE_W:ROUTE_LANE_W + 1]
    for k in range(1, TOP_K):
        moe = moe + gbuf_ref[k] * route[:, ROUTE_LANE_W + k:ROUTE_LANE_W + k + 1]
    x = x_ref[...] + _row(gate_ref) * moe
    o_ref[...] = x
    if final:
        y_ref[...] = _rmsnorm(x, fg_ref[...])


def _combine(dest, route, x_mid, mod, ys, final_g):
    t, d = x_mid.shape
    tm = min(ROW_DMA_TILE, t)
    final = final_g is not None
    row = lambda w: pl.BlockSpec((tm, w), lambda i: (i, 0))
    in_specs = [pl.BlockSpec((TOP_K * tm,), lambda i: (i,), memory_space=pltpu.SMEM),
                row(LANES), row(d), mod.spec(5, tm)]
    args = [dest, route, x_mid, mod.arr]
    if final:
        in_specs.append(pl.BlockSpec((1, d), lambda i: (0, 0)))
        args.append(final_g.reshape(1, d))
    in_specs.append(pl.BlockSpec(memory_space=pl.ANY))
    args.append(ys)
    out_shape = [jax.ShapeDtypeStruct((t, d), F32)] * (2 if final else 1)
    out_specs = [row(d)] * (2 if final else 1)
    return pl.pallas_call(
        functools.partial(_combine_kernel, final=final),
        out_shape=out_shape,
        grid=(t // tm,),
        in_specs=in_specs,
        out_specs=out_specs,
        scratch_shapes=[pltpu.VMEM((TOP_K, tm, d), F32), pltpu.SemaphoreType.DMA],
        compiler_params=_params(8 * tm * d * 4 + TOP_K * tm * d * 4, 1),
        name="moe_combine",
    )(*args)


def _moe(layer, h_p, h_s, logit_p, logit_s, xmid_p, xmid_s, mod_p, mod_s, w_gate, w_up, w_down, final_g):
    tp, d = h_p.shape
    ts = h_s.shape[0]
    route, counts = _route(jnp.concatenate([logit_p, logit_s], axis=0))
    cnt = counts[0, ROUTER_LANE0:ROUTER_LANE0 + N_EXPERTS].astype(I32)
    padded = (cnt + MOE_TILE - 1) // MOE_TILE * MOE_TILE
    ends = jnp.cumsum(padded)
    starts = ends - padded
    rec = route[:, :SUBLANES].T
    expert = rec[ROUTE_LANE_E:ROUTE_LANE_E + TOP_K].astype(I32)
    rank = rec[ROUTE_LANE_R:ROUTE_LANE_R + TOP_K].astype(I32)
    dest = starts[expert] + rank

    def by_tile(a):
        tm = min(ROW_DMA_TILE, a.shape[1])
        return jnp.transpose(a.reshape(TOP_K, a.shape[1] // tm, tm), (1, 0, 2)).reshape(-1)

    dest_p, dest_s = by_tile(dest[:, :tp]), by_tile(dest[:, tp:])
    n_rows = (tp + ts) * TOP_K + N_EXPERTS * MOE_TILE
    n_tiles = n_rows // MOE_TILE
    n_used = (ends[-1:] // MOE_TILE).astype(I32)
    tile_start = jnp.arange(n_tiles, dtype=I32) * MOE_TILE
    tile_expert = jnp.minimum(jnp.sum((ends[None, :] <= tile_start[:, None]).astype(I32), axis=1), N_EXPERTS - 1)
    xs = jnp.zeros((n_rows, d), F32)
    xs = _dispatch(h_p, dest_p, xs)
    xs = _dispatch(h_s, dest_s, xs)
    ys = _experts(xs, tile_expert, n_used, w_gate, w_up, w_down, layer)
    out_p = _combine(dest_p, route[:tp], xmid_p, mod_p, ys, final_g)
    out_s = _combine(dest_s, route[tp:], xmid_s, mod_s, ys, final_g)
    return out_p, out_s


def kernel(x_prompt, x_sample, c_prompt, c_sample, state_ssm, state_ssd_conv, cache_k, cache_v, page_table, state_short_conv, ada_w, ada_b, ln_mix_g, ln_ffn_g, mix_w_in, mix_w_out, ssd_conv_w, ssd_conv_b, ssd_dt_bias, ssd_a_log, ssd_d, ssd_norm_g, sb_bias, sc_w_in, sc_conv_w, sc_w_out, moe_w_group, moe_b_group, moe_w_router, moe_b_router, moe_w_gate, moe_w_up, moe_w_down, final_g):
    b, s, d = x_prompt.shape
    bs, ls, _ = x_sample.shape
    tp, ts = b * s, bs * ls
    depth = ada_w.shape[0]
    xp = x_prompt.reshape(tp, d)
    xs = x_sample.reshape(ts, d)
    mod = _adaln(jnp.concatenate([c_prompt, c_sample], axis=0), ada_w, ada_b)
    ls_pad = SUBLANES

    def pad_rows(a, rows, front=0):
        return jnp.pad(a, ((0, 0), (front, rows - a.shape[1] - front), (0, 0)))

    ssm_p, ssm_s, cv_p, cv_s, k_p, k_s, v_p, v_s, sc_p, sc_s = ([] for _ in range(10))
    y_out = None
    for layer in range(depth):
        i = layer // 2
        mod_p = _Mod(mod[layer, :b].reshape(b, 1, 6 * d), s, d)
        mod_s = _Mod(jnp.repeat(mod[layer, b:], ls, axis=0), ls, d)
        n_route = N_EXPERT_GROUPS + N_EXPERTS
        w_route = jnp.pad(jnp.concatenate([moe_w_group[layer], moe_w_router[layer]], axis=1),
                          ((0, 0), (0, LANES - n_route)))
        b_route = jnp.pad(jnp.concatenate([moe_b_group[layer], moe_b_router[layer]]),
                          (0, LANES - n_route)).reshape(1, LANES)
        if layer % 2 == 0:
            w = mix_w_in[i]
            o_dt = D_SSM + CONV_DIM
            o_q = o_dt + SSM_HEADS
            w_bf = jnp.concatenate(
                [w[:, :o_dt], w[:, o_q:o_q + D_ATT] * SB_SCALE, w[:, o_q + D_ATT:], w[:, o_dt:o_q],
                 jnp.zeros((d, LANES - SSM_HEADS), F32)], axis=1).astype(BF16)
            c0 = D_SSM + CONV_DIM
            splits = lambda kv, zt: ((0, D_SSM, (zt,)), (D_SSM, CONV_DIM, (F32,)), (c0, D_ATT, (BF16,)),
                                     (c0 + D_ATT, D_ATT, (kv, BF16)), (c0 + 2 * D_ATT, D_ATT, (kv, BF16)),
                                     (c0 + 3 * D_ATT, LANES, (F32,)))
            z1, xbc1, q1, kt1, kb1, vt1, vb1, dt1 = _pre_mix(
                xp, ln_mix_g[layer], mod_p, w_bf, splits(POSITION_LAST, BF16), "pre_mix_even_p")
            z2, xbc2, q2, kf2, kb2, vf2, vb2, dt2 = _pre_mix(
                xs, ln_mix_g[layer], mod_s, w_bf, splits(F32, F32), "pre_mix_even_s")
            prm = (ssd_conv_w[i], ssd_conv_b[i].reshape(1, CONV_DIM),
                   jnp.pad(ssd_dt_bias[i], (0, LANES - SSM_HEADS)).reshape(1, LANES),
                   jnp.pad(ssd_a_log[i], (0, LANES - SSM_HEADS)).reshape(1, LANES),
                   jnp.repeat(ssd_d[i], SSM_HEAD_DIM).reshape(1, D_SSM),
                   ssd_norm_g[i].reshape(1, D_SSM))
            y1, hfin1, cv1 = _ssd(xbc1, z1, dt1, prm, b, SSD_CHUNK, SSD_CHUNK, name="ssd_prompt")
            seq_pad = lambda a: pad_rows(a.reshape(bs, ls, a.shape[-1]), ls_pad).reshape(bs * ls_pad, a.shape[-1])
            y2, hfin2, cv2 = _ssd(
                seq_pad(xbc2), seq_pad(z2), seq_pad(dt2), prm, bs, ls_pad, ls,
                tail0=pad_rows(state_ssd_conv[i], SUBLANES, front=SUBLANES - (SSD_CONV - 1)),
                h0=state_ssm[i].reshape(bs, D_SSM, D_STATE), name="ssd_sample")
            y2 = y2.reshape(bs, ls_pad, D_SSM)[:, :ls].reshape(ts, D_SSM)
            att1 = _attn_prompt(q1, kb1, vb1, sb_bias[i], b)
            pos_last = lambda a: jnp.transpose(a, (0, 1, 3, 4, 2))
            att2 = _attn_sample(
                q2.reshape(bs, ls, D_ATT), pad_rows(kf2.reshape(bs, ls, D_ATT), SUBLANES),
                pad_rows(vf2.reshape(bs, ls, D_ATT), SUBLANES),
                pos_last(cache_k), pos_last(cache_v), i, page_table, sb_bias[i]).reshape(ts, D_ATT)
            w_out = mix_w_out[i].astype(BF16)
            w_list = [w_out[:D_SSM], w_out[D_SSM:]]
            xm1, hf1, lg1 = _post_mix([y1, att1], w_list, xp, ln_ffn_g[layer], mod_p, w_route, b_route, "post_mix_even_p")
            xm2, hf2, lg2 = _post_mix([y2, att2], w_list, xs, ln_ffn_g[layer], mod_s, w_route, b_route, "post_mix_even_s")
            ssm_p.append(hfin1.reshape(b, SSM_HEADS, SSM_HEAD_DIM, D_STATE))
            ssm_s.append(hfin2.reshape(bs, SSM_HEADS, SSM_HEAD_DIM, D_STATE))
            cv_p.append(cv1)
            cv_s.append(cv2)
            seq_first = lambda a: jnp.transpose(a.reshape(b, ATT_HEADS, ATT_HEAD_DIM, s), (0, 3, 1, 2))
            k_p.append(seq_first(kt1))
            k_s.append(kf2.reshape(bs, ls, ATT_HEADS, ATT_HEAD_DIM))
            v_p.append(seq_first(vt1))
            v_s.append(vf2.reshape(bs, ls, ATT_HEADS, ATT_HEAD_DIM))
        else:
            w_bf = sc_w_in[i].astype(BF16)
            splits = tuple((j * d, d, (BF16,)) for j in range(3))
            bg1, cg1, u1 = _pre_mix(xp, ln_mix_g[layer], mod_p, w_bf, splits, "pre_mix_odd_p")
            bg2, cg2, u2 = _pre_mix(xs, ln_mix_g[layer], mod_s, w_bf, splits, "pre_mix_odd_s")
            a1, st1 = _sc_prompt(bg1, cg1, u1, sc_conv_w[i], b)
            a2, st2 = _sc_sample(bg2, cg2, u2, state_short_conv[i], sc_conv_w[i], bs)
            w_list = [sc_w_out[i].astype(BF16)]
            xm1, hf1, lg1 = _post_mix([a1], w_list, xp, ln_ffn_g[layer], mod_p, w_route, b_route, "post_mix_odd_p")
            xm2, hf2, lg2 = _post_mix([a2], w_list, xs, ln_ffn_g[layer], mod_s, w_route, b_route, "post_mix_odd_s")
            sc_p.append(st1)
            sc_s.append(st2)
        fg = final_g if layer == depth - 1 else None
        out_p, out_s = _moe(layer, hf1, hf2, lg1, lg2, xm1, xm2, mod_p, mod_s,
                            moe_w_gate, moe_w_up, moe_w_down, fg)
        xp, xs = out_p[0], out_s[0]
        if fg is not None:
            y_out = (out_p[1].reshape(b, s, d), out_s[1].reshape(bs, ls, d))
    return (y_out[0], y_out[1], jnp.stack(ssm_p), jnp.stack(ssm_s), jnp.stack(cv_p), jnp.stack(cv_s),
            jnp.stack(k_p), jnp.stack(k_s), jnp.stack(v_p), jnp.stack(v_s), jnp.stack(sc_p), jnp.stack(sc_s))
```

```python
import functools

import jax
import jax.numpy as jnp
from jax import lax
from jax.experimental import pallas as pl
from jax.experimental.pallas import tpu as pltpu

F32 = jnp.float32
BF16 = jnp.bfloat16
I32 = jnp.int32

SSM_HEADS = 16
SSM_HEAD_DIM = 64
D_SSM = SSM_HEADS * SSM_HEAD_DIM
SSM_GROUPS = 2
HEADS_PER_GROUP = SSM_HEADS // SSM_GROUPS
D_STATE = 128
SSD_CONV = 4
SSD_CHUNK = 128
CONV_DIM = D_SSM + 2 * SSM_GROUPS * D_STATE
ATT_HEADS = 16
ATT_HEAD_DIM = 64
D_ATT = ATT_HEADS * ATT_HEAD_DIM
SB_SCALE = ATT_HEAD_DIM ** -0.5
SC_WIDTH = 3
N_EXPERT_GROUPS = 4
EXPERTS_PER_GROUP = 8
N_EXPERTS = N_EXPERT_GROUPS * EXPERTS_PER_GROUP
TOP_K = 2
D_EXPERT = 256
RMS_EPS = 1e-6
NEG_LOG2_E = -1.4426950408889634

LANES = 128
SUBLANES = 8
VMEM_BYTES_V7X = 64 * 1024 * 1024
VMEM_LIMIT_CAP = VMEM_BYTES_V7X - 8 * 1024 * 1024

TOKEN_TILE = 256
ATT_TILE = 256
ATT_HEAD_BLOCK = 4
MOE_TILE = 256
ROW_DMA_TILE = 512
ROW_DMA_UNROLL = 8
ROUTE_LANE_E = 0
ROUTE_LANE_R = 2
ROUTE_LANE_W = 4
ROUTER_LANE0 = N_EXPERT_GROUPS
POSITION_LAST = "f32, position axis last"


def _vmem_limit(nbytes):
    return int(min(max(2 * nbytes, 32 * 1024 * 1024), VMEM_LIMIT_CAP))


def _params(nbytes, n_axes):
    return pltpu.CompilerParams(dimension_semantics=("arbitrary",) * n_axes,
                                vmem_limit_bytes=_vmem_limit(nbytes))


def _dot(a, b):
    return jnp.dot(a, b, preferred_element_type=F32)


def _dot_nt(a, b):
    return lax.dot_general(a, b, (((1,), (1,)), ((), ())), preferred_element_type=F32)


def _dot_tn(a, b):
    return lax.dot_general(a, b, (((0,), (0,)), ((), ())), preferred_element_type=F32)


def _split2(x):
    hi = x.astype(BF16)
    lo = (x - hi.astype(F32)).astype(BF16)
    return hi, lo


def _split3(x):
    hi = x.astype(BF16)
    r = x - hi.astype(F32)
    mid = r.astype(BF16)
    lo = (r - mid.astype(F32)).astype(BF16)
    return hi, mid, lo


def _dot3(a, b):
    ah, al = _split2(a)
    bh, bl = _split2(b)
    return _dot(ah, bh) + (_dot(al, bh) + _dot(ah, bl))


def _dot4(a, b):
    m, n = a.shape[0], b.shape[1]
    r = _dot(jnp.concatenate(_split2(a), axis=0), jnp.concatenate(_split2(b), axis=1))
    return (r[:m, :n] + r[m:, n:]) + (r[:m, n:] + r[m:, :n])


def _silu(x):
    return x / (1.0 + jnp.exp(-x))


def _softplus(x):
    return jnp.maximum(x, 0.0) + jnp.log(1.0 + jnp.exp2(jnp.abs(x) * NEG_LOG2_E))


def _rmsnorm(x, g):
    return x * lax.rsqrt(jnp.mean(x * x, axis=-1, keepdims=True) + RMS_EPS) * g


def _row(ref):
    v = ref[...]
    return v[0] if v.ndim == 3 else v


def _adaln_kernel(c_ref, w_ref, b_ref, o_ref):
    o_ref[...] = _dot3(_silu(c_ref[...]), w_ref[...]) + b_ref[...]


def _adaln(c_all, ada_w, ada_b):
    depth, d, n = ada_w.shape
    rows = c_all.shape[0]
    tn = 1024
    return pl.pallas_call(
        _adaln_kernel,
        out_shape=jax.ShapeDtypeStruct((depth, rows, n), F32),
        grid=(depth, n // tn),
        in_specs=[pl.BlockSpec((rows, d), lambda l, j: (0, 0)),
                  pl.BlockSpec((None, d, tn), lambda l, j: (l, 0, j)),
                  pl.BlockSpec((None, 1, tn), lambda l, j: (l, 0, j))],
        out_specs=pl.BlockSpec((None, rows, tn), lambda l, j: (l, 0, j)),
        compiler_params=_params(2 * d * tn * 4 + 4 * rows * (d + tn) * 4, 2),
        name="adaln",
    )(c_all, ada_w, ada_b.reshape(depth, 1, n))


class _Mod:
    def __init__(self, arr, seq_len, d):
        self.arr = arr
        self.seq_len = seq_len
        self.d = d

    def spec(self, which, tm):
        d = self.d
        if self.arr.ndim == 3:
            tiles_per_seq = self.seq_len // tm
            return pl.BlockSpec((1, 1, d), lambda i, *_: (i // tiles_per_seq, 0, which))
        return pl.BlockSpec((tm, d), lambda i, *_: (i, which))


def _pre_mix_kernel(x_ref, g_ref, shift_ref, scale_ref, w_ref, *out_refs, splits):
    h = (_rmsnorm(x_ref[...], g_ref[...]) * (1.0 + _row(scale_ref)) + _row(shift_ref)).astype(BF16)
    k = 0
    for start, width, dtypes in splits:
        acc = _dot(h, w_ref[:, start:start + width])
        for dt in dtypes:
            if dt == POSITION_LAST:
                out_refs[k][...] = acc.T
            else:
                out_refs[k][...] = acc.astype(dt)
            k += 1


def _pre_mix(x, g, mod, w_bf, splits, name):
    t, d = x.shape
    n = w_bf.shape[1]
    tm = TOKEN_TILE
    tiles_per_seq = mod.seq_len // tm
    out_shape, out_specs, out_bytes = [], [], 0
    for _, width, dtypes in splits:
        for dt in dtypes:
            if dt == POSITION_LAST:
                out_shape.append(jax.ShapeDtypeStruct((t // mod.seq_len, width, mod.seq_len), F32))
                out_specs.append(pl.BlockSpec((None, width, tm), lambda i: (i // tiles_per_seq, 0, i % tiles_per_seq)))
                out_bytes += tm * width * 4
            else:
                out_shape.append(jax.ShapeDtypeStruct((t, width), dt))
                out_specs.append(pl.BlockSpec((tm, width), lambda i: (i, 0)))
                out_bytes += tm * width * jnp.dtype(dt).itemsize
    return pl.pallas_call(
        functools.partial(_pre_mix_kernel, splits=splits),
        out_shape=out_shape,
        grid=(t // tm,),
        in_specs=[pl.BlockSpec((tm, d), lambda i: (i, 0)),
                  pl.BlockSpec((1, d), lambda i: (0, 0)),
                  mod.spec(0, tm), mod.spec(1, tm),
                  pl.BlockSpec((d, n), lambda i: (0, 0))],
        out_specs=out_specs,
        compiler_params=_params(2 * d * n * 2 + 2 * tm * d * 4 + 2 * out_bytes + tm * n * 4, 1),
        name=name,
    )(x, g.reshape(1, d), mod.arr, mod.arr, w_bf)


def _ssd_kernel(*refs, chunk, n_real, has_init):
    if has_init:
        (xbc_ref, z_ref, dt_ref, tail0_ref, h0_ref, cw_ref, cb_ref, dtb_ref, alog_ref, dsk_ref, ng_ref,
         y_ref, hout_ref, cvout_ref, buf_ref, state_ref, ybuf_ref) = refs
    else:
        (xbc_ref, z_ref, dt_ref, cw_ref, cb_ref, dtb_ref, alog_ref, dsk_ref, ng_ref,
         y_ref, hout_ref, cvout_ref, buf_ref, state_ref, ybuf_ref) = refs
    c = pl.program_id(1)
    last = pl.num_programs(1) - 1
    L = chunk
    T0 = SUBLANES

    @pl.when(c == 0)
    def _():
        if has_init:
            buf_ref[0:T0, :] = tail0_ref[0]
            state_ref[...] = h0_ref[0]
        else:
            buf_ref[0:T0, :] = jnp.zeros((T0, CONV_DIM), F32)
            state_ref[...] = jnp.zeros_like(state_ref)

    xbc = xbc_ref[...]
    buf_ref[T0:T0 + L, :] = xbc
    x_ext = buf_ref[0:T0 + L, :]
    conv = xbc * cw_ref[SSD_CONV - 1:SSD_CONV, :]
    for j in range(SSD_CONV - 1):
        shifted = pltpu.roll(x_ext, SSD_CONV - 1 - j, axis=0)[T0:]
        conv = conv + shifted * cw_ref[j:j + 1, :]
    xc = _silu(conv + cb_ref[...])
    xs = xc[:, :D_SSM]
    bmat = xc[:, D_SSM:D_SSM + SSM_GROUPS * D_STATE].astype(BF16)
    cmat = xc[:, D_SSM + SSM_GROUPS * D_STATE:].astype(BF16)

    @pl.when(c == last)
    def _():
        cvout_ref[0] = buf_ref[pl.ds(T0 + n_real - (SSD_CONV - 1), SSD_CONV - 1), :]

    if n_real == L:
        buf_ref[0:T0, :] = buf_ref[L:L + T0, :]

    dt = _softplus(dt_ref[...] + dtb_ref[...])
    rows = lax.broadcasted_iota(I32, (L, L), 0)
    cols = lax.broadcasted_iota(I32, (L, L), 1)
    if n_real < L:
        dt = jnp.where(lax.broadcasted_iota(I32, dt.shape, 0) < n_real, dt, 0.0)
    a = -jnp.exp(alog_ref[...])
    tri = cols <= rows
    tri_bf = jnp.where(tri, 1.0, 0.0).astype(BF16)
    da_hi, da_mid, da_lo = _split3(dt * a)
    acum = _dot(tri_bf, da_hi) + (_dot(tri_bf, da_mid) + _dot(tri_bf, da_lo))
    acum_t = acum.T
    a_last = acum[L - 1:L, :]
    chunk_decay = jnp.exp(a_last)
    per_head = jnp.concatenate([dt, jnp.exp(acum), jnp.exp(a_last - acum)], axis=0)
    head_of_lane = lax.broadcasted_iota(I32, (3 * LANES, D_SSM), 1) // SSM_HEAD_DIM
    expand3 = jnp.where(lax.broadcasted_iota(I32, (3 * LANES, D_SSM), 0) % LANES == head_of_lane, 1.0, 0.0).astype(BF16)
    wide = _dot(jnp.concatenate(_split3(per_head), axis=1), expand3)
    dt_w, e_acum_w, to_end_w = wide[:L], wide[L:2 * L], wide[2 * L:]
    xdt = xs * dt_w
    xdt_bf = xdt.astype(BF16)
    x_end = (xdt * to_end_w).astype(BF16)

    gw = HEADS_PER_GROUP * SSM_HEAD_DIM
    for g in range(SSM_GROUPS):
        gl = slice(g * gw, (g + 1) * gw)
        gs = slice(g * D_STATE, (g + 1) * D_STATE)
        st = state_ref[gl, :]
        ybuf_ref[:, gl] = _dot_nt(cmat[:, gs], st.astype(BF16)) * e_acum_w[:, gl] + xs[:, gl] * dsk_ref[:, gl]
        s_new = _dot_tn(x_end[:, gl], bmat[:, gs])
        for h in range(g * HEADS_PER_GROUP, (g + 1) * HEADS_PER_GROUP):
            hs = slice(h * SSM_HEAD_DIM, (h + 1) * SSM_HEAD_DIM)
            hg = slice(hs.start - gl.start, hs.stop - gl.start)
            state_ref[hs, :] = st[hg] * chunk_decay[:, h:h + 1] + s_new[hg]

    cb = [_dot_nt(cmat[:, g * D_STATE:(g + 1) * D_STATE], bmat[:, g * D_STATE:(g + 1) * D_STATE])
          for g in range(SSM_GROUPS)]
    first_half = lax.broadcasted_iota(I32, (1, LANES), 1) < SSM_HEAD_DIM
    for pair in range(SSM_HEADS // 2):
        g = (2 * pair) // HEADS_PER_GROUP
        pl_ = slice(pair * LANES, (pair + 1) * LANES)
        ms = []
        for h in (2 * pair, 2 * pair + 1):
            seg = acum[:, h:h + 1] - acum_t[h:h + 1, :]
            ms.append((cb[g] * jnp.exp(jnp.where(tri, seg, -jnp.inf))).astype(BF16))
        xp = xdt_bf[:, pl_]
        zero = jnp.zeros_like(xp)
        x_bd = jnp.concatenate([jnp.where(first_half, xp, zero), jnp.where(first_half, zero, xp)], axis=0)
        ybuf_ref[:, pl_] += _dot(jnp.concatenate(ms, axis=1), x_bd)

    y = ybuf_ref[...] * _silu(z_ref[...].astype(F32))
    half = D_SSM // SSM_GROUPS
    for g in range(SSM_GROUPS):
        yg = y[:, g * half:(g + 1) * half]
        y_ref[:, g * half:(g + 1) * half] = (
            yg * lax.rsqrt(jnp.mean(yg * yg, axis=-1, keepdims=True) + RMS_EPS) * ng_ref[:, g * half:(g + 1) * half]
        ).astype(y_ref.dtype)

    @pl.when(c == last)
    def _():
        hout_ref[0] = state_ref[...]


def _ssd(xbc, z, dt, prm, n_seq, chunk, n_real, tail0=None, h0=None, name="ssd"):
    t = xbc.shape[0]
    nc = t // (n_seq * chunk)
    has_init = tail0 is not None
    cw, cb, dtb, alog, dsk, ng = prm
    row_spec = lambda w: pl.BlockSpec((chunk, w), lambda b, c: (b * nc + c, 0))
    const = lambda shp: pl.BlockSpec(shp, lambda b, c: (0,) * len(shp))
    in_specs = [row_spec(CONV_DIM), row_spec(D_SSM), row_spec(LANES)]
    args = [xbc, z, dt]
    if has_init:
        in_specs += [pl.BlockSpec((1, SUBLANES, CONV_DIM), lambda b, c: (b, 0, 0)),
                     pl.BlockSpec((1, D_SSM, D_STATE), lambda b, c: (b, 0, 0))]
        args += [tail0, h0]
    in_specs += [const((SSD_CONV, CONV_DIM)), const((1, CONV_DIM)), const((1, LANES)), const((1, LANES)),
                 const((1, D_SSM)), const((1, D_SSM))]
    args += [cw, cb, dtb, alog, dsk, ng]
    scratch_bytes = ((chunk + 2 * SUBLANES) * CONV_DIM + D_SSM * D_STATE + chunk * D_SSM) * 4
    block_bytes = chunk * (CONV_DIM + D_SSM + LANES) * 4 + chunk * D_SSM * 2 + 2 * D_SSM * D_STATE * 4
    return pl.pallas_call(
        functools.partial(_ssd_kernel, chunk=chunk, n_real=n_real, has_init=has_init),
        out_shape=[jax.ShapeDtypeStruct((t, D_SSM), BF16),
                   jax.ShapeDtypeStruct((n_seq, D_SSM, D_STATE), F32),
                   jax.ShapeDtypeStruct((n_seq, SSD_CONV - 1, CONV_DIM), F32)],
        grid=(n_seq, nc),
        in_specs=in_specs,
        out_specs=[row_spec(D_SSM),
                   pl.BlockSpec((1, D_SSM, D_STATE), lambda b, c: (b, 0, 0)),
                   pl.BlockSpec((1, SSD_CONV - 1, CONV_DIM), lambda b, c: (b, 0, 0))],
        scratch_shapes=[pltpu.VMEM((chunk + 2 * SUBLANES, CONV_DIM), F32),
                        pltpu.VMEM((D_SSM, D_STATE), F32),
                        pltpu.VMEM((chunk, D_SSM), F32)],
        compiler_params=_params(2 * block_bytes + scratch_bytes + 8 * chunk * CONV_DIM * 4, 2),
        name=name,
    )(*args)


MASKED_LOGIT = -1e30


def _sb_scores(s, neg_upper):
    sp = _softplus(s)
    if neg_upper.shape[0] == 2 * s.shape[1]:
        later = _dot(jnp.concatenate(_split2(sp), axis=1), neg_upper)
        return s - sp, later, later[:, 0:1] - sp[:, 0:1]
    later = _dot(sp.astype(BF16), neg_upper)
    return s - sp, later, -jnp.sum(sp, axis=-1, keepdims=True)


def _neg_strict_upper(n, copies):
    j = lax.broadcasted_iota(I32, (copies * n, n), 0) % n
    k = lax.broadcasted_iota(I32, (copies * n, n), 1)
    return jnp.where(j > k, -1.0, 0.0).astype(BF16)


def _attn_prompt_kernel(bias_ref, q_ref, k_ref, v_ref, o_ref, qs_ref, carry_ref, acc_ref, s_ref):
    hq = pl.program_id(1)
    qi = pl.program_id(2)
    tq, width = q_ref.shape
    tk = ATT_TILE
    nh = ATT_HEAD_BLOCK
    m = nh * tq
    lane_head = lax.broadcasted_iota(I32, (1, width), 1) // ATT_HEAD_DIM
    q = q_ref[...]
    for h in range(nh):
        qs_ref[h * tq:(h + 1) * tq, :] = jnp.where(lane_head == h, q, jnp.zeros_like(q))
    row_head = lax.broadcasted_iota(I32, (m, 1), 0) // tq
    bias = jnp.zeros((m, 1), F32)
    for h in range(nh):
        bias = jnp.where(row_head == h, bias_ref[hq * nh + h], bias)
    neg_upper = _neg_strict_upper(tk, 1)
    causal = lax.broadcasted_iota(I32, (m, tk), 1) < lax.broadcasted_iota(I32, (m, tk), 0) % tq

    def scores(j):
        return _dot_nt(qs_ref[...], k_ref[pl.ds(pl.multiple_of(j * tk, tk), tk), :])

    def tile(j, mask):
        vt = v_ref[pl.ds(pl.multiple_of(j * tk, tk), tk), :]
        s = s_ref[...] + bias
        s_ref[...] = scores(jnp.maximum(j - 1, 0))
        if mask is not None:
            s = jnp.where(mask, s, MASKED_LOGIT)
        log_beta, later, total = _sb_scores(s, neg_upper)
        carry = carry_ref[...]
        w = jnp.exp(log_beta + (later + carry)).astype(BF16)
        carry_ref[...] = carry + total
        w_cat = jnp.concatenate([w[h * tq:(h + 1) * tq] for h in range(nh)], axis=1)
        v_cat = jnp.concatenate([jnp.where(lane_head == h, vt, jnp.zeros_like(vt)) for h in range(nh)], axis=0)
        acc_ref[...] += _dot(w_cat, v_cat)

    carry_ref[...] = jnp.zeros_like(carry_ref)
    acc_ref[...] = jnp.zeros_like(acc_ref)
    s_ref[...] = scores(qi)
    tile(qi, causal)

    def body(i, c):
        tile(qi - 1 - i, None)
        return c

    lax.fori_loop(0, qi, body, 0)
    o_ref[...] = acc_ref[...].astype(o_ref.dtype)


def _attn_prompt(q, k, v, bias, n_seq):
    t, d = q.shape
    s = t // n_seq
    tq = ATT_TILE
    nq = s // tq
    width = ATT_HEAD_BLOCK * ATT_HEAD_DIM
    return pl.pallas_call(
        _attn_prompt_kernel,
        out_shape=jax.ShapeDtypeStruct((t, d), BF16),
        grid_spec=pltpu.PrefetchScalarGridSpec(
            num_scalar_prefetch=1,
            grid=(n_seq, d // width, nq),
            in_specs=[pl.BlockSpec((tq, width), lambda b, hq, qi, *_: (b * nq + qi, hq)),
                      pl.BlockSpec((s, width), lambda b, hq, qi, *_: (b, hq)),
                      pl.BlockSpec((s, width), lambda b, hq, qi, *_: (b, hq))],
            out_specs=pl.BlockSpec((tq, width), lambda b, hq, qi, *_: (b * nq + qi, hq)),
            scratch_shapes=[pltpu.VMEM((ATT_HEAD_BLOCK * tq, width), BF16),
                            pltpu.VMEM((ATT_HEAD_BLOCK * tq, 1), F32),
                            pltpu.VMEM((tq, width), F32),
                            pltpu.VMEM((ATT_HEAD_BLOCK * tq, ATT_TILE), F32)]),
        compiler_params=_params(4 * s * width * 2 + 4 * tq * width * 2
                                + 12 * ATT_HEAD_BLOCK * tq * ATT_TILE * 4, 3),
        name="sb_attn_prompt",
    )(bias, q, k, v)


def _attn_sample_kernel(pt_ref, bias_ref, q_ref, kn_ref, vn_ref, *refs, n_pages, n_new):
    k_refs = refs[:n_pages]
    v_refs = refs[n_pages:2 * n_pages]
    o_ref = refs[2 * n_pages]
    d = q_ref.shape[-1]
    page = k_refs[0].shape[-1]
    n_rows = n_new * ATT_HEADS
    n_blocks = n_pages + 1
    m = n_blocks * n_rows
    row_head = lax.broadcasted_iota(I32, (n_rows, 1), 0) % ATT_HEADS
    lane_head = lax.broadcasted_iota(I32, (1, d), 1) // ATT_HEAD_DIM
    own = lane_head == row_head
    q = q_ref[0]
    qx = jnp.broadcast_to(q[:, None, :], (n_new, ATT_HEADS, d)).reshape(n_rows, d)
    qx = jnp.where(own, qx, jnp.zeros_like(qx))
    bias = jnp.zeros((n_rows, 1), F32)
    for h in range(ATT_HEADS):
        bias = jnp.where(row_head == h, bias_ref[h], bias)

    pad = jnp.zeros((page - kn_ref.shape[1], d), F32)
    k_new = jnp.concatenate([kn_ref[0], pad], axis=0).astype(BF16)
    v_new = jnp.concatenate([vn_ref[0], pad], axis=0).astype(BF16)
    order = range(n_pages - 1, -1, -1)
    kt = jnp.concatenate([k_refs[p][...].reshape(d, page).astype(BF16) for p in order], axis=1)
    vt = jnp.concatenate([v_refs[p][...].reshape(d, page).astype(BF16) for p in order], axis=1)

    s_pages = _dot(qx, kt)
    s = jnp.concatenate([_dot_nt(qx, k_new)] + [s_pages[:, i * page:(i + 1) * page] for i in range(n_pages)], axis=0)
    s = s + jnp.concatenate([bias] * n_blocks, axis=0)
    row = lax.broadcasted_iota(I32, (m, page), 0)
    col = lax.broadcasted_iota(I32, (m, page), 1)
    s = jnp.where((row >= n_rows) | (col < row // ATT_HEADS), s, MASKED_LOGIT)
    log_beta, later, total = _sb_scores(s, _neg_strict_upper(page, 2))
    carry = jnp.zeros((n_rows, 1), F32)
    carries = []
    for i in range(n_blocks):
        carries.append(carry)
        carry = carry + total[i * n_rows:(i + 1) * n_rows]
    w = jnp.exp(log_beta + (later + jnp.concatenate(carries, axis=0))).astype(BF16)
    w_pages = jnp.concatenate([w[(i + 1) * n_rows:(i + 2) * n_rows] for i in range(n_pages)], axis=1)
    acc = _dot(w[:n_rows], v_new) + _dot_nt(w_pages, vt)
    acc = jnp.where(own, acc, 0.0)
    o_ref[0] = jnp.sum(acc.reshape(n_new, ATT_HEADS, d), axis=1).astype(o_ref.dtype)


def _attn_sample(q, k_new, v_new, cache_kt, cache_vt, layer, page_table, bias):
    bs, n_new, d = q.shape
    n_pages = page_table.shape[1]
    _, _, n_heads, dh, page = cache_kt.shape
    page_spec = lambda p: pl.BlockSpec((None, None, n_heads, dh, page),
                                       lambda b, pt, bias_: (layer, pt[b, p], 0, 0, 0))
    seq_spec = lambda rows: pl.BlockSpec((1, rows, d), lambda b, pt, bias_: (b, 0, 0))
    page_bytes = page * d * 4
    return pl.pallas_call(
        functools.partial(_attn_sample_kernel, n_pages=n_pages, n_new=n_new),
        out_shape=jax.ShapeDtypeStruct((bs, n_new, d), BF16),
        grid_spec=pltpu.PrefetchScalarGridSpec(
            num_scalar_prefetch=2,
            grid=(bs,),
            in_specs=[seq_spec(n_new), seq_spec(k_new.shape[1]), seq_spec(v_new.shape[1])]
            + [page_spec(p) for p in range(n_pages)] * 2,
            out_specs=seq_spec(n_new)),
        compiler_params=_params(2 * 2 * n_pages * page_bytes + 8 * page_bytes, 1),
        name="sb_attn_sample",
    )(page_table, bias, q, k_new, v_new, *([cache_kt] * n_pages), *([cache_vt] * n_pages))


def _sc_prompt_kernel(b_ref, c_ref, u_ref, w_ref, a_ref, st_ref, buf_ref):
    L = c_ref.shape[0]
    T0 = SUBLANES
    cu = c_ref[...].astype(F32) * u_ref[...].astype(F32)
    buf_ref[0:T0, :] = jnp.zeros((T0, cu.shape[1]), F32)
    buf_ref[T0:T0 + L, :] = cu
    conv = cu * w_ref[SC_WIDTH - 1:SC_WIDTH, :]
    for j in range(SC_WIDTH - 1):
        conv = conv + buf_ref[pl.ds(T0 - (SC_WIDTH - 1) + j, L), :] * w_ref[j:j + 1, :]
    a_ref[...] = (b_ref[...].astype(F32) * conv).astype(a_ref.dtype)
    st_ref[0] = buf_ref[pl.ds(T0 + L - (SC_WIDTH - 1), SC_WIDTH - 1), :]


def _sc_prompt(bg, cg, u, conv_w, n_seq):
    t, d = cg.shape
    s = t // n_seq
    wl = 256
    spec = pl.BlockSpec((s, wl), lambda b, j: (b, j))
    return pl.pallas_call(
        _sc_prompt_kernel,
        out_shape=[jax.ShapeDtypeStruct((t, d), BF16),
                   jax.ShapeDtypeStruct((n_seq, SC_WIDTH - 1, d), F32)],
        grid=(n_seq, d // wl),
        in_specs=[spec, spec, spec, pl.BlockSpec((SC_WIDTH, wl), lambda b, j: (0, j))],
        out_specs=[spec, pl.BlockSpec((1, SC_WIDTH - 1, wl), lambda b, j: (b, 0, j))],
        scratch_shapes=[pltpu.VMEM((s + SUBLANES, wl), F32)],
        compiler_params=_params(2 * 3 * s * wl * 4 + 2 * s * wl * 2 + 6 * s * wl * 4, 2),
        name="short_conv_prompt",
    )(bg, cg, u, conv_w)


def _sc_sample_kernel(b_ref, c_ref, u_ref, st_ref, w_ref, a_ref, sto_ref, *, n_new, d):
    up = [st_ref[:, j * d:(j + 1) * d] for j in range(SC_WIDTH - 1)]
    up += [c_ref[:, t * d:(t + 1) * d].astype(F32) * u_ref[:, t * d:(t + 1) * d].astype(F32) for t in range(n_new)]
    for t in range(n_new):
        conv = up[t] * w_ref[0:1, :]
        for j in range(1, SC_WIDTH):
            conv = conv + up[t + j] * w_ref[j:j + 1, :]
        a_ref[:, t * d:(t + 1) * d] = (b_ref[:, t * d:(t + 1) * d].astype(F32) * conv).astype(a_ref.dtype)
    for j in range(SC_WIDTH - 1):
        sto_ref[:, j * d:(j + 1) * d] = up[n_new + j]


def _sc_sample(bg, cg, u, state, conv_w, n_seq):
    t, d = cg.shape
    n_new = t // n_seq
    wide = lambda x: x.reshape(n_seq, n_new * d)
    a, st = pl.pallas_call(
        functools.partial(_sc_sample_kernel, n_new=n_new, d=d),
        out_shape=[jax.ShapeDtypeStruct((n_seq, n_new * d), BF16),
                   jax.ShapeDtypeStruct((n_seq, (SC_WIDTH - 1) * d), F32)],
        compiler_params=pltpu.CompilerParams(vmem_limit_bytes=_vmem_limit(16 * n_seq * n_new * d * 4)),
        name="short_conv_sample",
    )(wide(bg), wide(cg), wide(u), state.reshape(n_seq, (SC_WIDTH - 1) * d), conv_w)
    return a.reshape(t, d), st.reshape(n_seq, SC_WIDTH - 1, d)


def _post_mix_kernel(*refs, n_in):
    a_refs = refs[:n_in]
    w_refs = refs[n_in:2 * n_in]
    x_ref, gate_ref, g_ref, shift_ref, scale_ref, wr_ref, br_ref, xmid_ref, h_ref, logit_ref = refs[2 * n_in:]
    acc = _dot(a_refs[0][...], w_refs[0][...])
    for a_ref, w_ref in zip(a_refs[1:], w_refs[1:]):
        acc = acc + _dot(a_ref[...], w_ref[...])
    xm = x_ref[...] + _row(gate_ref) * acc
    xmid_ref[...] = xm
    h = _rmsnorm(xm, g_ref[...]) * (1.0 + _row(scale_ref)) + _row(shift_ref)
    h_ref[...] = h
    logit_ref[...] = _dot4(h, wr_ref[...]) + br_ref[...]


def _post_mix(a_list, w_list, x, g, mod, w_route, b_route, name):
    t, d = x.shape
    tm = TOKEN_TILE
    n_in = len(a_list)
    row = lambda w: pl.BlockSpec((tm, w), lambda i: (i, 0))
    const = lambda shp: pl.BlockSpec(shp, lambda i: (0, 0))
    w_bytes = sum(w.size * 2 for w in w_list)
    return pl.pallas_call(
        functools.partial(_post_mix_kernel, n_in=n_in),
        out_shape=[jax.ShapeDtypeStruct((t, d), F32), jax.ShapeDtypeStruct((t, d), F32),
                   jax.ShapeDtypeStruct((t, LANES), F32)],
        grid=(t // tm,),
        in_specs=[row(a.shape[1]) for a in a_list] + [const(w.shape) for w in w_list]
        + [row(d), mod.spec(2, tm), const((1, d)), mod.spec(3, tm), mod.spec(4, tm), const((d, LANES)), const((1, LANES))],
        out_specs=[row(d), row(d), row(LANES)],
        compiler_params=_params(2 * w_bytes + 2 * tm * d * (2 * n_in + 12) + 8 * tm * d * 4, 1),
        name=name,
    )(*a_list, *w_list, x, mod.arr, g.reshape(1, d), mod.arr, mod.arr, w_route, b_route)


def _route_kernel(lg_ref, route_ref, cnt_ref, acc_ref):
    i = pl.program_id(0)
    tm = lg_ref.shape[0]

    @pl.when(i == 0)
    def _():
        acc_ref[...] = jnp.zeros_like(acc_ref)

    lg = lg_ref[...]
    lane = lax.broadcasted_iota(I32, lg.shape, 1)
    neg = -jnp.inf
    gl = jnp.where(lane < N_EXPERT_GROUPS, lg, neg)
    gmax = jnp.max(gl, axis=-1, keepdims=True)
    gidx = jnp.min(jnp.where(gl == gmax, lane, LANES), axis=-1, keepdims=True)
    g_top = 1.0 / jnp.sum(jnp.exp(gl - gmax), axis=-1, keepdims=True)
    lo = ROUTER_LANE0 + gidx * EXPERTS_PER_GROUP
    el = jnp.where((lane >= lo) & (lane < lo + EXPERTS_PER_GROUP), lg, neg)
    m1 = jnp.max(el, axis=-1, keepdims=True)
    i1 = jnp.min(jnp.where(el == m1, lane, LANES), axis=-1, keepdims=True)
    el2 = jnp.where(lane == i1, neg, el)
    m2 = jnp.max(el2, axis=-1, keepdims=True)
    i2 = jnp.min(jnp.where(el2 == m2, lane, LANES), axis=-1, keepdims=True)
    p2 = jnp.exp(m2 - m1)
    w1 = g_top / (1.0 + p2)
    w2 = w1 * p2
    sel1 = lane == i1
    sel2 = lane == i2
    onehot = jnp.where(sel1 | sel2, 1.0, 0.0).astype(BF16)
    r = lax.broadcasted_iota(I32, (tm, tm), 0)
    c = lax.broadcasted_iota(I32, (tm, tm), 1)
    tri = jnp.where(c <= r, 1.0, 0.0).astype(BF16)
    cum = _dot(tri, onehot) + acc_ref[...]
    r1 = jnp.sum(jnp.where(sel1, cum, 0.0), axis=-1, keepdims=True) - 1.0
    r2 = jnp.sum(jnp.where(sel2, cum, 0.0), axis=-1, keepdims=True) - 1.0
    acc_ref[...] = cum[tm - 1:tm, :]
    cnt_ref[...] = cum[tm - 1:tm, :]
    e1 = (i1 - ROUTER_LANE0).astype(F32)
    e2 = (i2 - ROUTER_LANE0).astype(F32)
    rec = jnp.zeros(lg.shape, F32)
    for k, val in enumerate((e1, e2, r1, r2, w1, w2)):
        rec = jnp.where(lane == k, val, rec)
    route_ref[...] = rec


def _route(logits):
    t = logits.shape[0]
    tm = TOKEN_TILE
    return pl.pallas_call(
        _route_kernel,
        out_shape=[jax.ShapeDtypeStruct((t, LANES), F32), jax.ShapeDtypeStruct((1, LANES), F32)],
        grid=(t // tm,),
        in_specs=[pl.BlockSpec((tm, LANES), lambda i: (i, 0))],
        out_specs=[pl.BlockSpec((tm, LANES), lambda i: (i, 0)), pl.BlockSpec((1, LANES), lambda i: (0, 0))],
        scratch_shapes=[pltpu.VMEM((1, LANES), F32)],
        compiler_params=_params(64 * tm * LANES * 4, 1),
        name="moe_route",
    )(logits)


def _row_copy(src_ref, src_row, dst_ref, dst_row, sem):
    return pltpu.make_async_copy(src_ref.at[pl.ds(src_row, 1)], dst_ref.at[pl.ds(dst_row, 1)], sem)


def _dispatch_kernel(dest_ref, h_ref, xs_in_ref, xs_ref, sem):
    del xs_in_ref
    tm = h_ref.shape[0]

    def start(r, _):
        for k in range(TOP_K):
            _row_copy(h_ref, r, xs_ref, dest_ref[k * tm + r], sem).start(priority=k % 2)
        return 0

    lax.fori_loop(0, tm, start, 0, unroll=ROW_DMA_UNROLL)
    for k in range(TOP_K):
        pltpu.make_async_copy(h_ref, xs_ref.at[pl.ds(0, tm)], sem).wait()


def _dispatch(h, dest, xs):
    t, d = h.shape
    tm = min(ROW_DMA_TILE, t)
    return pl.pallas_call(
        _dispatch_kernel,
        out_shape=jax.ShapeDtypeStruct(xs.shape, xs.dtype),
        grid=(t // tm,),
        in_specs=[pl.BlockSpec((TOP_K * tm,), lambda i: (i,), memory_space=pltpu.SMEM),
                  pl.BlockSpec((tm, d), lambda i: (i, 0)),
                  pl.BlockSpec(memory_space=pl.ANY)],
        out_specs=pl.BlockSpec(memory_space=pl.ANY),
        scratch_shapes=[pltpu.SemaphoreType.DMA],
        input_output_aliases={2: 0},
        compiler_params=_params(4 * tm * d * 4, 1),
        name="moe_dispatch",
    )(dest, h, xs)


def _experts_kernel(te_ref, nu_ref, xs_ref, wg_ref, wu_ref, wd_ref, ys_ref):
    i = pl.program_id(0)

    @pl.when(i < nu_ref[0])
    def _():
        x = xs_ref[...].astype(BF16)
        hg = _dot(x, wg_ref[...].astype(BF16))
        hu = _dot(x, wu_ref[...].astype(BF16))
        hid = (_silu(hg) * hu).astype(BF16)
        ys_ref[...] = _dot(hid, wd_ref[...].astype(BF16))

    @pl.when(i >= nu_ref[0])
    def _():
        ys_ref[...] = jnp.zeros_like(ys_ref)


def _experts(xs, tile_expert, n_used, w_gate, w_up, w_down, layer):
    p, d = xs.shape
    tm = MOE_TILE
    f = w_gate.shape[-1]
    row_map = lambda i, te, nu: (jnp.minimum(i, jnp.maximum(nu[0] - 1, 0)), 0)
    return pl.pallas_call(
        _experts_kernel,
        out_shape=jax.ShapeDtypeStruct((p, d), F32),
        grid_spec=pltpu.PrefetchScalarGridSpec(
            num_scalar_prefetch=2,
            grid=(p // tm,),
            in_specs=[pl.BlockSpec((tm, d), row_map),
                      pl.BlockSpec((None, None, d, f), lambda i, te, nu: (layer, te[i], 0, 0)),
                      pl.BlockSpec((None, None, d, f), lambda i, te, nu: (layer, te[i], 0, 0)),
                      pl.BlockSpec((None, None, f, d), lambda i, te, nu: (layer, te[i], 0, 0))],
            out_specs=pl.BlockSpec((tm, d), lambda i, te, nu: (i, 0))),
        compiler_params=_params(2 * 3 * d * f * 4 + 4 * tm * d * 4 + 3 * d * f * 2 + 8 * tm * f * 4, 1),
        name="moe_experts",
    )(tile_expert, n_used, xs, w_gate, w_up, w_down)


def _combine_kernel(*refs, final):
    if final:
        dest_ref, route_ref, x_ref, gate_ref, fg_ref, ys_ref, o_ref, y_ref, gbuf_ref, sem = refs
    else:
        dest_ref, route_ref, x_ref, gate_ref, ys_ref, o_ref, gbuf_ref, sem = refs
    tm = x_ref.shape[0]

    def start(r, _):
        for k in range(TOP_K):
            _row_copy(ys_ref, dest_ref[k * tm + r], gbuf_ref.at[k], r, sem).start(priority=k % 2)
        return 0

    lax.fori_loop(0, tm, start, 0, unroll=ROW_DMA_UNROLL)
    for k in range(TOP_K):
        pltpu.make_async_copy(ys_ref.at[pl.ds(0, tm)], gbuf_ref.at[k], sem).wait()
    route = route_ref[...]
    moe = gbuf_ref[0] * route[:, ROUTE_LANE_W:ROUTE_LANE_W + 1]
    for k in range(1, TOP_K):
        moe = moe + gbuf_ref[k] * route[:, ROUTE_LANE_W + k:ROUTE_LANE_W + k + 1]
    x = x_ref[...] + _row(gate_ref) * moe
    o_ref[...] = x
    if final:
        y_ref[...] = _rmsnorm(x, fg_ref[...])


def _combine(dest, route, x_mid, mod, ys, final_g):
    t, d = x_mid.shape
    tm = min(ROW_DMA_TILE, t)
    final = final_g is not None
    row = lambda w: pl.BlockSpec((tm, w), lambda i: (i, 0))
    in_specs = [pl.BlockSpec((TOP_K * tm,), lambda i: (i,), memory_space=pltpu.SMEM),
                row(LANES), row(d), mod.spec(5, tm)]
    args = [dest, route, x_mid, mod.arr]
    if final:
        in_specs.append(pl.BlockSpec((1, d), lambda i: (0, 0)))
        args.append(final_g.reshape(1, d))
    in_specs.append(pl.BlockSpec(memory_space=pl.ANY))
    args.append(ys)
    out_shape = [jax.ShapeDtypeStruct((t, d), F32)] * (2 if final else 1)
    out_specs = [row(d)] * (2 if final else 1)
    return pl.pallas_call(
        functools.partial(_combine_kernel, final=final),
        out_shape=out_shape,
        grid=(t // tm,),
        in_specs=in_specs,
        out_specs=out_specs,
        scratch_shapes=[pltpu.VMEM((TOP_K, tm, d), F32), pltpu.SemaphoreType.DMA],
        compiler_params=_params(8 * tm * d * 4 + TOP_K * tm * d * 4, 1),
        name="moe_combine",
    )(*args)


def _moe(layer, h_p, h_s, logit_p, logit_s, xmid_p, xmid_s, mod_p, mod_s, w_gate, w_up, w_down, final_g):
    tp, d = h_p.shape
    ts = h_s.shape[0]
    route, counts = _route(jnp.concatenate([logit_p, logit_s], axis=0))
    cnt = counts[0, ROUTER_LANE0:ROUTER_LANE0 + N_EXPERTS].astype(I32)
    padded = (cnt + MOE_TILE - 1) // MOE_TILE * MOE_TILE
    ends = jnp.cumsum(padded)
    starts = ends - padded
    rec = route[:, :SUBLANES].T
    expert = rec[ROUTE_LANE_E:ROUTE_LANE_E + TOP_K].astype(I32)
    rank = rec[ROUTE_LANE_R:ROUTE_LANE_R + TOP_K].astype(I32)
    is_e = expert[None] == jnp.arange(N_EXPERTS, dtype=I32)[:, None, None]
    dest = jnp.sum(jnp.where(is_e, starts[:, None, None], 0), axis=0) + rank

    def by_tile(a):
        tm = min(ROW_DMA_TILE, a.shape[1])
        return jnp.transpose(a.reshape(TOP_K, a.shape[1] // tm, tm), (1, 0, 2)).reshape(-1)

    dest_p, dest_s = by_tile(dest[:, :tp]), by_tile(dest[:, tp:])
    n_rows = (tp + ts) * TOP_K + N_EXPERTS * MOE_TILE
    n_tiles = n_rows // MOE_TILE
    n_used = (ends[-1:] // MOE_TILE).astype(I32)
    tile_start = jnp.arange(n_tiles, dtype=I32) * MOE_TILE
    tile_expert = jnp.minimum(jnp.sum((ends[None, :] <= tile_start[:, None]).astype(I32), axis=1), N_EXPERTS - 1)
    xs = jnp.zeros((n_rows, d), F32)
    xs = _dispatch(h_p, dest_p, xs)
    xs = _dispatch(h_s, dest_s, xs)
    ys = _experts(xs, tile_expert, n_used, w_gate, w_up, w_down, layer)
    out_p = _combine(dest_p, route[:tp], xmid_p, mod_p, ys, final_g)
    out_s = _combine(dest_s, route[tp:], xmid_s, mod_s, ys, final_g)
    return out_p, out_s


def kernel(x_prompt, x_sample, c_prompt, c_sample, state_ssm, state_ssd_conv, cache_k, cache_v, page_table, state_short_conv, ada_w, ada_b, ln_mix_g, ln_ffn_g, mix_w_in, mix_w_out, ssd_conv_w, ssd_conv_b, ssd_dt_bias, ssd_a_log, ssd_d, ssd_norm_g, sb_bias, sc_w_in, sc_conv_w, sc_w_out, moe_w_group, moe_b_group, moe_w_router, moe_b_router, moe_w_gate, moe_w_up, moe_w_down, final_g):
    b, s, d = x_prompt.shape
    bs, ls, _ = x_sample.shape
    tp, ts = b * s, bs * ls
    depth = ada_w.shape[0]
    xp = x_prompt.reshape(tp, d)
    xs = x_sample.reshape(ts, d)
    mod = _adaln(jnp.concatenate([c_prompt, c_sample], axis=0), ada_w, ada_b)
    ls_pad = SUBLANES

    def pad_rows(a, rows, front=0):
        return jnp.pad(a, ((0, 0), (front, rows - a.shape[1] - front), (0, 0)))

    ssm_p, ssm_s, cv_p, cv_s, k_p, k_s, v_p, v_s, sc_p, sc_s = ([] for _ in range(10))
    y_out = None
    for layer in range(depth):
        i = layer // 2
        mod_p = _Mod(mod[layer, :b].reshape(b, 1, 6 * d), s, d)
        mod_s = _Mod(jnp.repeat(mod[layer, b:], ls, axis=0), ls, d)
        n_route = N_EXPERT_GROUPS + N_EXPERTS
        w_route = jnp.pad(jnp.concatenate([moe_w_group[layer], moe_w_router[layer]], axis=1),
                          ((0, 0), (0, LANES - n_route)))
        b_route = jnp.pad(jnp.concatenate([moe_b_group[layer], moe_b_router[layer]]),
                          (0, LANES - n_route)).reshape(1, LANES)
        if layer % 2 == 0:
            w = mix_w_in[i]
            o_dt = D_SSM + CONV_DIM
            o_q = o_dt + SSM_HEADS
            w_bf = jnp.concatenate(
                [w[:, :o_dt], w[:, o_q:o_q + D_ATT] * SB_SCALE, w[:, o_q + D_ATT:], w[:, o_dt:o_q],
                 jnp.zeros((d, LANES - SSM_HEADS), F32)], axis=1).astype(BF16)
            c0 = D_SSM + CONV_DIM
            splits = lambda kv, zt: ((0, D_SSM, (zt,)), (D_SSM, CONV_DIM, (F32,)), (c0, D_ATT, (BF16,)),
                                     (c0 + D_ATT, D_ATT, (kv, BF16)), (c0 + 2 * D_ATT, D_ATT, (kv, BF16)),
                                     (c0 + 3 * D_ATT, LANES, (F32,)))
            z1, xbc1, q1, kt1, kb1, vt1, vb1, dt1 = _pre_mix(
                xp, ln_mix_g[layer], mod_p, w_bf, splits(POSITION_LAST, BF16), "pre_mix_even_p")
            z2, xbc2, q2, kf2, kb2, vf2, vb2, dt2 = _pre_mix(
                xs, ln_mix_g[layer], mod_s, w_bf, splits(F32, F32), "pre_mix_even_s")
            prm = (ssd_conv_w[i], ssd_conv_b[i].reshape(1, CONV_DIM),
                   jnp.pad(ssd_dt_bias[i], (0, LANES - SSM_HEADS)).reshape(1, LANES),
                   jnp.pad(ssd_a_log[i], (0, LANES - SSM_HEADS)).reshape(1, LANES),
                   jnp.repeat(ssd_d[i], SSM_HEAD_DIM).reshape(1, D_SSM),
                   ssd_norm_g[i].reshape(1, D_SSM))
            y1, hfin1, cv1 = _ssd(xbc1, z1, dt1, prm, b, SSD_CHUNK, SSD_CHUNK, name="ssd_prompt")
            seq_pad = lambda a: pad_rows(a.reshape(bs, ls, a.shape[-1]), ls_pad).reshape(bs * ls_pad, a.shape[-1])
            y2, hfin2, cv2 = _ssd(
                seq_pad(xbc2), seq_pad(z2), seq_pad(dt2), prm, bs, ls_pad, ls,
                tail0=pad_rows(state_ssd_conv[i], SUBLANES, front=SUBLANES - (SSD_CONV - 1)),
                h0=state_ssm[i].reshape(bs, D_SSM, D_STATE), name="ssd_sample")
            y2 = y2.reshape(bs, ls_pad, D_SSM)[:, :ls].reshape(ts, D_SSM)
            att1 = _attn_prompt(q1, kb1, vb1, sb_bias[i], b)
            pos_last = lambda a: jnp.transpose(a, (0, 1, 3, 4, 2))
            att2 = _attn_sample(
                q2.reshape(bs, ls, D_ATT), pad_rows(kf2.reshape(bs, ls, D_ATT), SUBLANES),
                pad_rows(vf2.reshape(bs, ls, D_ATT), SUBLANES),
                pos_last(cache_k), pos_last(cache_v), i, page_table, sb_bias[i]).reshape(ts, D_ATT)
            w_out = mix_w_out[i].astype(BF16)
            w_list = [w_out[:D_SSM], w_out[D_SSM:]]
            xm1, hf1, lg1 = _post_mix([y1, att1], w_list, xp, ln_ffn_g[layer], mod_p, w_route, b_route, "post_mix_even_p")
            xm2, hf2, lg2 = _post_mix([y2, att2], w_list, xs, ln_ffn_g[layer], mod_s, w_route, b_route, "post_mix_even_s")
            ssm_p.append(hfin1.reshape(b, SSM_HEADS, SSM_HEAD_DIM, D_STATE))
            ssm_s.append(hfin2.reshape(bs, SSM_HEADS, SSM_HEAD_DIM, D_STATE))
            cv_p.append(cv1)
            cv_s.append(cv2)
            seq_first = lambda a: jnp.transpose(a.reshape(b, ATT_HEADS, ATT_HEAD_DIM, s), (0, 3, 1, 2))
            k_p.append(seq_first(kt1))
            k_s.append(kf2.reshape(bs, ls, ATT_HEADS, ATT_HEAD_DIM))
            v_p.append(seq_first(vt1))
            v_s.append(vf2.reshape(bs, ls, ATT_HEADS, ATT_HEAD_DIM))
        else:
            w_bf = sc_w_in[i].astype(BF16)
            splits = tuple((j * d, d, (BF16,)) for j in range(3))
            bg1, cg1, u1 = _pre_mix(xp, ln_mix_g[layer], mod_p, w_bf, splits, "pre_mix_odd_p")
            bg2, cg2, u2 = _pre_mix(xs, ln_mix_g[layer], mod_s, w_bf, splits, "pre_mix_odd_s")
            a1, st1 = _sc_prompt(bg1, cg1, u1, sc_conv_w[i], b)
            a2, st2 = _sc_sample(bg2, cg2, u2, state_short_conv[i], sc_conv_w[i], bs)
            w_list = [sc_w_out[i].astype(BF16)]
            xm1, hf1, lg1 = _post_mix([a1], w_list, xp, ln_ffn_g[layer], mod_p, w_route, b_route, "post_mix_odd_p")
            xm2, hf2, lg2 = _post_mix([a2], w_list, xs, ln_ffn_g[layer], mod_s, w_route, b_route, "post_mix_odd_s")
            sc_p.append(st1)
            sc_s.append(st2)
        fg = final_g if layer == depth - 1 else None
        out_p, out_s = _moe(layer, hf1, hf2, lg1, lg2, xm1, xm2, mod_p, mod_s,
                            moe_w_gate, moe_w_up, moe_w_down, fg)
        xp, xs = out_p[0], out_s[0]
        if fg is not None:
            y_out = (out_p[1].reshape(b, s, d), out_s[1].reshape(bs, ls, d))
    return (y_out[0], y_out[1], jnp.stack(ssm_p), jnp.stack(ssm_s), jnp.stack(cv_p), jnp.stack(cv_s),
            jnp.stack(k_p), jnp.stack(k_s), jnp.stack(v_p), jnp.stack(v_s), jnp.stack(sc_p), jnp.stack(sc_s))
```

```python
import functools

import jax
import jax.numpy as jnp
from jax import lax
from jax.experimental import pallas as pl
from jax.experimental.pallas import tpu as pltpu

F32 = jnp.float32
BF16 = jnp.bfloat16
I32 = jnp.int32

SSM_HEADS = 16
SSM_HEAD_DIM = 64
D_SSM = SSM_HEADS * SSM_HEAD_DIM
SSM_GROUPS = 2
HEADS_PER_GROUP = SSM_HEADS // SSM_GROUPS
D_STATE = 128
SSD_CONV = 4
SSD_CHUNK = 128
CONV_DIM = D_SSM + 2 * SSM_GROUPS * D_STATE
ATT_HEADS = 16
ATT_HEAD_DIM = 64
D_ATT = ATT_HEADS * ATT_HEAD_DIM
SB_SCALE = ATT_HEAD_DIM ** -0.5
SC_WIDTH = 3
N_EXPERT_GROUPS = 4
EXPERTS_PER_GROUP = 8
N_EXPERTS = N_EXPERT_GROUPS * EXPERTS_PER_GROUP
TOP_K = 2
D_EXPERT = 256
RMS_EPS = 1e-6
NEG_LOG2_E = -1.4426950408889634

LANES = 128
SUBLANES = 8
VMEM_BYTES_V7X = 64 * 1024 * 1024
VMEM_LIMIT_CAP = VMEM_BYTES_V7X - 8 * 1024 * 1024

TOKEN_TILE = 256
ATT_TILE = 256
ATT_HEAD_BLOCK = 4
MOE_TILE = 256
ROW_DMA_TILE = 512
ROW_DMA_UNROLL = 8
ROUTE_LANE_E = 0
ROUTE_LANE_R = 2
ROUTE_LANE_W = 4
ROUTER_LANE0 = N_EXPERT_GROUPS
POSITION_LAST = "f32, position axis last"


def _vmem_limit(nbytes):
    return int(min(max(2 * nbytes, 32 * 1024 * 1024), VMEM_LIMIT_CAP))


def _params(nbytes, n_axes):
    return pltpu.CompilerParams(dimension_semantics=("arbitrary",) * n_axes,
                                vmem_limit_bytes=_vmem_limit(nbytes))


def _dot(a, b):
    return jnp.dot(a, b, preferred_element_type=F32)


def _dot_nt(a, b):
    return lax.dot_general(a, b, (((1,), (1,)), ((), ())), preferred_element_type=F32)


def _dot_tn(a, b):
    return lax.dot_general(a, b, (((0,), (0,)), ((), ())), preferred_element_type=F32)


def _split2(x):
    hi = x.astype(BF16)
    lo = (x - hi.astype(F32)).astype(BF16)
    return hi, lo


def _split3(x):
    hi = x.astype(BF16)
    r = x - hi.astype(F32)
    mid = r.astype(BF16)
    lo = (r - mid.astype(F32)).astype(BF16)
    return hi, mid, lo


def _dot3(a, b):
    ah, al = _split2(a)
    bh, bl = _split2(b)
    return _dot(ah, bh) + (_dot(al, bh) + _dot(ah, bl))


def _dot4(a, b):
    m, n = a.shape[0], b.shape[1]
    r = _dot(jnp.concatenate(_split2(a), axis=0), jnp.concatenate(_split2(b), axis=1))
    return (r[:m, :n] + r[m:, n:]) + (r[:m, n:] + r[m:, :n])


def _silu(x):
    return x / (1.0 + jnp.exp(-x))


def _softplus(x):
    return jnp.maximum(x, 0.0) + jnp.log(1.0 + jnp.exp2(jnp.abs(x) * NEG_LOG2_E))


def _rmsnorm(x, g):
    return x * lax.rsqrt(jnp.mean(x * x, axis=-1, keepdims=True) + RMS_EPS) * g


def _row(ref):
    v = ref[...]
    return v[0] if v.ndim == 3 else v


def _adaln_kernel(c_ref, w_ref, b_ref, o_ref):
    o_ref[...] = _dot3(_silu(c_ref[...]), w_ref[...]) + b_ref[...]


def _adaln(c_all, ada_w, ada_b):
    depth, d, n = ada_w.shape
    rows = c_all.shape[0]
    tn = 1024
    return pl.pallas_call(
        _adaln_kernel,
        out_shape=jax.ShapeDtypeStruct((depth, rows, n), F32),
        grid=(depth, n // tn),
        in_specs=[pl.BlockSpec((rows, d), lambda l, j: (0, 0)),
                  pl.BlockSpec((None, d, tn), lambda l, j: (l, 0, j)),
                  pl.BlockSpec((None, 1, tn), lambda l, j: (l, 0, j))],
        out_specs=pl.BlockSpec((None, rows, tn), lambda l, j: (l, 0, j)),
        compiler_params=_params(2 * d * tn * 4 + 4 * rows * (d + tn) * 4, 2),
        name="adaln",
    )(c_all, ada_w, ada_b.reshape(depth, 1, n))


class _Mod:
    def __init__(self, arr, seq_len, d):
        self.arr = arr
        self.seq_len = seq_len
        self.d = d

    def spec(self, which, tm):
        d = self.d
        if self.arr.ndim == 3:
            tiles_per_seq = self.seq_len // tm
            return pl.BlockSpec((1, 1, d), lambda i, *_: (i // tiles_per_seq, 0, which))
        return pl.BlockSpec((tm, d), lambda i, *_: (i, which))


def _pre_mix_kernel(x_ref, g_ref, shift_ref, scale_ref, w_ref, *out_refs, splits):
    h = (_rmsnorm(x_ref[...], g_ref[...]) * (1.0 + _row(scale_ref)) + _row(shift_ref)).astype(BF16)
    k = 0
    for start, width, dtypes in splits:
        acc = _dot(h, w_ref[:, start:start + width])
        for dt in dtypes:
            if dt == POSITION_LAST:
                out_refs[k][...] = acc.T
            else:
                out_refs[k][...] = acc.astype(dt)
            k += 1


def _pre_mix(x, g, mod, w_bf, splits, name):
    t, d = x.shape
    n = w_bf.shape[1]
    tm = TOKEN_TILE
    tiles_per_seq = mod.seq_len // tm
    out_shape, out_specs, out_bytes = [], [], 0
    for _, width, dtypes in splits:
        for dt in dtypes:
            if dt == POSITION_LAST:
                out_shape.append(jax.ShapeDtypeStruct((t // mod.seq_len, width, mod.seq_len), F32))
                out_specs.append(pl.BlockSpec((None, width, tm), lambda i: (i // tiles_per_seq, 0, i % tiles_per_seq)))
                out_bytes += tm * width * 4
            else:
                out_shape.append(jax.ShapeDtypeStruct((t, width), dt))
                out_specs.append(pl.BlockSpec((tm, width), lambda i: (i, 0)))
                out_bytes += tm * width * jnp.dtype(dt).itemsize
    return pl.pallas_call(
        functools.partial(_pre_mix_kernel, splits=splits),
        out_shape=out_shape,
        grid=(t // tm,),
        in_specs=[pl.BlockSpec((tm, d), lambda i: (i, 0)),
                  pl.BlockSpec((1, d), lambda i: (0, 0)),
                  mod.spec(0, tm), mod.spec(1, tm),
                  pl.BlockSpec((d, n), lambda i: (0, 0))],
        out_specs=out_specs,
        compiler_params=_params(2 * d * n * 2 + 2 * tm * d * 4 + 2 * out_bytes + tm * n * 4, 1),
        name=name,
    )(x, g.reshape(1, d), mod.arr, mod.arr, w_bf)


def _ssd_kernel(*refs, chunk, n_real, has_init):
    if has_init:
        (xbc_ref, z_ref, dt_ref, tail0_ref, h0_ref, cw_ref, cb_ref, dtb_ref, alog_ref, dsk_ref, ng_ref,
         y_ref, hout_ref, cvout_ref, buf_ref, state_ref, ybuf_ref) = refs
    else:
        (xbc_ref, z_ref, dt_ref, cw_ref, cb_ref, dtb_ref, alog_ref, dsk_ref, ng_ref,
         y_ref, hout_ref, cvout_ref, buf_ref, state_ref, ybuf_ref) = refs
    c = pl.program_id(1)
    last = pl.num_programs(1) - 1
    L = chunk
    T0 = SUBLANES

    @pl.when(c == 0)
    def _():
        if has_init:
            buf_ref[0:T0, :] = tail0_ref[0]
            state_ref[...] = h0_ref[0]
        else:
            buf_ref[0:T0, :] = jnp.zeros((T0, CONV_DIM), F32)
            state_ref[...] = jnp.zeros_like(state_ref)

    xbc = xbc_ref[...]
    buf_ref[T0:T0 + L, :] = xbc
    x_ext = buf_ref[0:T0 + L, :]
    conv = xbc * cw_ref[SSD_CONV - 1:SSD_CONV, :]
    for j in range(SSD_CONV - 1):
        shifted = pltpu.roll(x_ext, SSD_CONV - 1 - j, axis=0)[T0:]
        conv = conv + shifted * cw_ref[j:j + 1, :]
    xc = _silu(conv + cb_ref[...])
    xs = xc[:, :D_SSM]
    bmat = xc[:, D_SSM:D_SSM + SSM_GROUPS * D_STATE].astype(BF16)
    cmat = xc[:, D_SSM + SSM_GROUPS * D_STATE:].astype(BF16)

    @pl.when(c == last)
    def _():
        cvout_ref[0] = buf_ref[pl.ds(T0 + n_real - (SSD_CONV - 1), SSD_CONV - 1), :]

    if n_real == L:
        buf_ref[0:T0, :] = buf_ref[L:L + T0, :]

    dt = _softplus(dt_ref[...] + dtb_ref[...])
    rows = lax.broadcasted_iota(I32, (L, L), 0)
    cols = lax.broadcasted_iota(I32, (L, L), 1)
    if n_real < L:
        dt = jnp.where(lax.broadcasted_iota(I32, dt.shape, 0) < n_real, dt, 0.0)
    a = -jnp.exp(alog_ref[...])
    tri = cols <= rows
    tri_bf = jnp.where(tri, 1.0, 0.0).astype(BF16)
    da_hi, da_mid, da_lo = _split3(dt * a)
    acum = _dot(tri_bf, da_hi) + (_dot(tri_bf, da_mid) + _dot(tri_bf, da_lo))
    acum_t = acum.T
    a_last = acum[L - 1:L, :]
    chunk_decay = jnp.exp(a_last)
    per_head = jnp.concatenate([dt, jnp.exp(acum), jnp.exp(a_last - acum)], axis=0)
    head_of_lane = lax.broadcasted_iota(I32, (3 * LANES, D_SSM), 1) // SSM_HEAD_DIM
    expand3 = jnp.where(lax.broadcasted_iota(I32, (3 * LANES, D_SSM), 0) % LANES == head_of_lane, 1.0, 0.0).astype(BF16)
    wide = _dot(jnp.concatenate(_split3(per_head), axis=1), expand3)
    dt_w, e_acum_w, to_end_w = wide[:L], wide[L:2 * L], wide[2 * L:]
    xdt = xs * dt_w
    xdt_bf = xdt.astype(BF16)
    x_end = (xdt * to_end_w).astype(BF16)

    gw = HEADS_PER_GROUP * SSM_HEAD_DIM
    for g in range(SSM_GROUPS):
        gl = slice(g * gw, (g + 1) * gw)
        gs = slice(g * D_STATE, (g + 1) * D_STATE)
        st = state_ref[gl, :]
        ybuf_ref[:, gl] = _dot_nt(cmat[:, gs], st.astype(BF16)) * e_acum_w[:, gl] + xs[:, gl] * dsk_ref[:, gl]
        s_new = _dot_tn(x_end[:, gl], bmat[:, gs])
        for h in range(g * HEADS_PER_GROUP, (g + 1) * HEADS_PER_GROUP):
            hs = slice(h * SSM_HEAD_DIM, (h + 1) * SSM_HEAD_DIM)
            hg = slice(hs.start - gl.start, hs.stop - gl.start)
            state_ref[hs, :] = st[hg] * chunk_decay[:, h:h + 1] + s_new[hg]

    cb = [_dot_nt(cmat[:, g * D_STATE:(g + 1) * D_STATE], bmat[:, g * D_STATE:(g + 1) * D_STATE])
          for g in range(SSM_GROUPS)]
    first_half = lax.broadcasted_iota(I32, (1, LANES), 1) < SSM_HEAD_DIM
    for pair in range(SSM_HEADS // 2):
        g = (2 * pair) // HEADS_PER_GROUP
        pl_ = slice(pair * LANES, (pair + 1) * LANES)
        ms = []
        for h in (2 * pair, 2 * pair + 1):
            seg = acum[:, h:h + 1] - acum_t[h:h + 1, :]
            ms.append((cb[g] * jnp.exp(jnp.where(tri, seg, -jnp.inf))).astype(BF16))
        xp = xdt_bf[:, pl_]
        zero = jnp.zeros_like(xp)
        x_bd = jnp.concatenate([jnp.where(first_half, xp, zero), jnp.where(first_half, zero, xp)], axis=0)
        ybuf_ref[:, pl_] += _dot(jnp.concatenate(ms, axis=1), x_bd)

    y = ybuf_ref[...] * _silu(z_ref[...].astype(F32))
    half = D_SSM // SSM_GROUPS
    for g in range(SSM_GROUPS):
        yg = y[:, g * half:(g + 1) * half]
        y_ref[:, g * half:(g + 1) * half] = (
            yg * lax.rsqrt(jnp.mean(yg * yg, axis=-1, keepdims=True) + RMS_EPS) * ng_ref[:, g * half:(g + 1) * half]
        ).astype(y_ref.dtype)

    @pl.when(c == last)
    def _():
        hout_ref[0] = state_ref[...]


def _ssd(xbc, z, dt, prm, n_seq, chunk, n_real, tail0=None, h0=None, name="ssd"):
    t = xbc.shape[0]
    nc = t // (n_seq * chunk)
    has_init = tail0 is not None
    cw, cb, dtb, alog, dsk, ng = prm
    row_spec = lambda w: pl.BlockSpec((chunk, w), lambda b, c: (b * nc + c, 0))
    const = lambda shp: pl.BlockSpec(shp, lambda b, c: (0,) * len(shp))
    in_specs = [row_spec(CONV_DIM), row_spec(D_SSM), row_spec(LANES)]
    args = [xbc, z, dt]
    if has_init:
        in_specs += [pl.BlockSpec((1, SUBLANES, CONV_DIM), lambda b, c: (b, 0, 0)),
                     pl.BlockSpec((1, D_SSM, D_STATE), lambda b, c: (b, 0, 0))]
        args += [tail0, h0]
    in_specs += [const((SSD_CONV, CONV_DIM)), const((1, CONV_DIM)), const((1, LANES)), const((1, LANES)),
                 const((1, D_SSM)), const((1, D_SSM))]
    args += [cw, cb, dtb, alog, dsk, ng]
    scratch_bytes = ((chunk + 2 * SUBLANES) * CONV_DIM + D_SSM * D_STATE + chunk * D_SSM) * 4
    block_bytes = chunk * (CONV_DIM + D_SSM + LANES) * 4 + chunk * D_SSM * 2 + 2 * D_SSM * D_STATE * 4
    return pl.pallas_call(
        functools.partial(_ssd_kernel, chunk=chunk, n_real=n_real, has_init=has_init),
        out_shape=[jax.ShapeDtypeStruct((t, D_SSM), BF16),
                   jax.ShapeDtypeStruct((n_seq, D_SSM, D_STATE), F32),
                   jax.ShapeDtypeStruct((n_seq, SSD_CONV - 1, CONV_DIM), F32)],
        grid=(n_seq, nc),
        in_specs=in_specs,
        out_specs=[row_spec(D_SSM),
                   pl.BlockSpec((1, D_SSM, D_STATE), lambda b, c: (b, 0, 0)),
                   pl.BlockSpec((1, SSD_CONV - 1, CONV_DIM), lambda b, c: (b, 0, 0))],
        scratch_shapes=[pltpu.VMEM((chunk + 2 * SUBLANES, CONV_DIM), F32),
                        pltpu.VMEM((D_SSM, D_STATE), F32),
                        pltpu.VMEM((chunk, D_SSM), F32)],
        compiler_params=_params(2 * block_bytes + scratch_bytes + 8 * chunk * CONV_DIM * 4, 2),
        name=name,
    )(*args)


MASKED_LOGIT = -1e30


def _sb_scores(s, neg_upper):
    sp = _softplus(s)
    if neg_upper.shape[0] == 2 * s.shape[1]:
        later = _dot(jnp.concatenate(_split2(sp), axis=1), neg_upper)
        return s - sp, later, later[:, 0:1] - sp[:, 0:1]
    later = _dot(sp.astype(BF16), neg_upper)
    return s - sp, later, -jnp.sum(sp, axis=-1, keepdims=True)


def _neg_strict_upper(n, copies):
    j = lax.broadcasted_iota(I32, (copies * n, n), 0) % n
    k = lax.broadcasted_iota(I32, (copies * n, n), 1)
    return jnp.where(j > k, -1.0, 0.0).astype(BF16)


def _attn_prompt_kernel(bias_ref, q_ref, k_ref, v_ref, o_ref, qs_ref, carry_ref, acc_ref, s_ref):
    hq = pl.program_id(1)
    qi = pl.program_id(2)
    tq, width = q_ref.shape
    tk = ATT_TILE
    nh = ATT_HEAD_BLOCK
    m = nh * tq
    lane_head = lax.broadcasted_iota(I32, (1, width), 1) // ATT_HEAD_DIM
    q = q_ref[...]
    for h in range(nh):
        qs_ref[h * tq:(h + 1) * tq, :] = jnp.where(lane_head == h, q, jnp.zeros_like(q))
    row_head = lax.broadcasted_iota(I32, (m, 1), 0) // tq
    bias = jnp.zeros((m, 1), F32)
    for h in range(nh):
        bias = jnp.where(row_head == h, bias_ref[hq * nh + h], bias)
    neg_upper = _neg_strict_upper(tk, 1)
    causal = lax.broadcasted_iota(I32, (m, tk), 1) < lax.broadcasted_iota(I32, (m, tk), 0) % tq

    def scores(j):
        return _dot_nt(qs_ref[...], k_ref[pl.ds(pl.multiple_of(j * tk, tk), tk), :])

    def tile(j, mask):
        vt = v_ref[pl.ds(pl.multiple_of(j * tk, tk), tk), :]
        s = s_ref[...] + bias
        s_ref[...] = scores(jnp.maximum(j - 1, 0))
        if mask is not None:
            s = jnp.where(mask, s, MASKED_LOGIT)
        log_beta, later, total = _sb_scores(s, neg_upper)
        carry = carry_ref[...]
        w = jnp.exp(log_beta + (later + carry)).astype(BF16)
        carry_ref[...] = carry + total
        w_cat = jnp.concatenate([w[h * tq:(h + 1) * tq] for h in range(nh)], axis=1)
        v_cat = jnp.concatenate([jnp.where(lane_head == h, vt, jnp.zeros_like(vt)) for h in range(nh)], axis=0)
        acc_ref[...] += _dot(w_cat, v_cat)

    carry_ref[...] = jnp.zeros_like(carry_ref)
    acc_ref[...] = jnp.zeros_like(acc_ref)
    s_ref[...] = scores(qi)
    tile(qi, causal)

    def body(i, c):
        tile(qi - 1 - i, None)
        return c

    lax.fori_loop(0, qi, body, 0)
    o_ref[...] = acc_ref[...].astype(o_ref.dtype)


def _attn_prompt(q, k, v, bias, n_seq):
    t, d = q.shape
    s = t // n_seq
    tq = ATT_TILE
    nq = s // tq
    width = ATT_HEAD_BLOCK * ATT_HEAD_DIM
    return pl.pallas_call(
        _attn_prompt_kernel,
        out_shape=jax.ShapeDtypeStruct((t, d), BF16),
        grid_spec=pltpu.PrefetchScalarGridSpec(
            num_scalar_prefetch=1,
            grid=(n_seq, d // width, nq),
            in_specs=[pl.BlockSpec((tq, width), lambda b, hq, qi, *_: (b * nq + qi, hq)),
                      pl.BlockSpec((s, width), lambda b, hq, qi, *_: (b, hq)),
                      pl.BlockSpec((s, width), lambda b, hq, qi, *_: (b, hq))],
            out_specs=pl.BlockSpec((tq, width), lambda b, hq, qi, *_: (b * nq + qi, hq)),
            scratch_shapes=[pltpu.VMEM((ATT_HEAD_BLOCK * tq, width), BF16),
                            pltpu.VMEM((ATT_HEAD_BLOCK * tq, 1), F32),
                            pltpu.VMEM((tq, width), F32),
                            pltpu.VMEM((ATT_HEAD_BLOCK * tq, ATT_TILE), F32)]),
        compiler_params=_params(4 * s * width * 2 + 4 * tq * width * 2
                                + 12 * ATT_HEAD_BLOCK * tq * ATT_TILE * 4, 3),
        name="sb_attn_prompt",
    )(bias, q, k, v)


def _attn_sample_kernel(pt_ref, bias_ref, q_ref, kn_ref, vn_ref, *refs, n_pages, n_new):
    k_refs = refs[:n_pages]
    v_refs = refs[n_pages:2 * n_pages]
    o_ref = refs[2 * n_pages]
    d = q_ref.shape[-1]
    page = k_refs[0].shape[-1]
    n_rows = n_new * ATT_HEADS
    n_blocks = n_pages + 1
    m = n_blocks * n_rows
    row_head = lax.broadcasted_iota(I32, (n_rows, 1), 0) % ATT_HEADS
    lane_head = lax.broadcasted_iota(I32, (1, d), 1) // ATT_HEAD_DIM
    own = lane_head == row_head
    q = q_ref[0]
    qx = jnp.broadcast_to(q[:, None, :], (n_new, ATT_HEADS, d)).reshape(n_rows, d)
    qx = jnp.where(own, qx, jnp.zeros_like(qx))
    bias = jnp.zeros((n_rows, 1), F32)
    for h in range(ATT_HEADS):
        bias = jnp.where(row_head == h, bias_ref[h], bias)

    pad = jnp.zeros((page - kn_ref.shape[1], d), F32)
    k_new = jnp.concatenate([kn_ref[0], pad], axis=0).astype(BF16)
    v_new = jnp.concatenate([vn_ref[0], pad], axis=0).astype(BF16)
    order = range(n_pages - 1, -1, -1)
    kt = jnp.concatenate([k_refs[p][...].reshape(d, page).astype(BF16) for p in order], axis=1)
    vt = jnp.concatenate([v_refs[p][...].reshape(d, page).astype(BF16) for p in order], axis=1)

    s_pages = _dot(qx, kt)
    s = jnp.concatenate([_dot_nt(qx, k_new)] + [s_pages[:, i * page:(i + 1) * page] for i in range(n_pages)], axis=0)
    s = s + jnp.concatenate([bias] * n_blocks, axis=0)
    row = lax.broadcasted_iota(I32, (m, page), 0)
    col = lax.broadcasted_iota(I32, (m, page), 1)
    s = jnp.where((row >= n_rows) | (col < row // ATT_HEADS), s, MASKED_LOGIT)
    log_beta, later, total = _sb_scores(s, _neg_strict_upper(page, 2))
    carry = jnp.zeros((n_rows, 1), F32)
    carries = []
    for i in range(n_blocks):
        carries.append(carry)
        carry = carry + total[i * n_rows:(i + 1) * n_rows]
    w = jnp.exp(log_beta + (later + jnp.concatenate(carries, axis=0))).astype(BF16)
    w_pages = jnp.concatenate([w[(i + 1) * n_rows:(i + 2) * n_rows] for i in range(n_pages)], axis=1)
    acc = _dot(w[:n_rows], v_new) + _dot_nt(w_pages, vt)
    acc = jnp.where(own, acc, 0.0)
    o_ref[0] = jnp.sum(acc.reshape(n_new, ATT_HEADS, d), axis=1).astype(o_ref.dtype)


def _attn_sample(q, k_new, v_new, cache_kt, cache_vt, layer, page_table, bias):
    bs, n_new, d = q.shape
    n_pages = page_table.shape[1]
    _, _, n_heads, dh, page = cache_kt.shape
    page_spec = lambda p: pl.BlockSpec((None, None, n_heads, dh, page),
                                       lambda b, pt, bias_: (layer, pt[b, p], 0, 0, 0))
    seq_spec = lambda rows: pl.BlockSpec((1, rows, d), lambda b, pt, bias_: (b, 0, 0))
    page_bytes = page * d * 4
    return pl.pallas_call(
        functools.partial(_attn_sample_kernel, n_pages=n_pages, n_new=n_new),
        out_shape=jax.ShapeDtypeStruct((bs, n_new, d), BF16),
        grid_spec=pltpu.PrefetchScalarGridSpec(
            num_scalar_prefetch=2,
            grid=(bs,),
            in_specs=[seq_spec(n_new), seq_spec(k_new.shape[1]), seq_spec(v_new.shape[1])]
            + [page_spec(p) for p in range(n_pages)] * 2,
            out_specs=seq_spec(n_new)),
        compiler_params=_params(2 * 2 * n_pages * page_bytes + 8 * page_bytes, 1),
        name="sb_attn_sample",
    )(page_table, bias, q, k_new, v_new, *([cache_kt] * n_pages), *([cache_vt] * n_pages))


def _sc_prompt_kernel(b_ref, c_ref, u_ref, w_ref, a_ref, st_ref, buf_ref):
    L = c_ref.shape[0]
    T0 = SUBLANES
    cu = c_ref[...].astype(F32) * u_ref[...].astype(F32)
    buf_ref[0:T0, :] = jnp.zeros((T0, cu.shape[1]), F32)
    buf_ref[T0:T0 + L, :] = cu
    conv = cu * w_ref[SC_WIDTH - 1:SC_WIDTH, :]
    for j in range(SC_WIDTH - 1):
        conv = conv + buf_ref[pl.ds(T0 - (SC_WIDTH - 1) + j, L), :] * w_ref[j:j + 1, :]
    a_ref[...] = (b_ref[...].astype(F32) * conv).astype(a_ref.dtype)
    st_ref[0] = buf_ref[pl.ds(T0 + L - (SC_WIDTH - 1), SC_WIDTH - 1), :]


def _sc_prompt(bg, cg, u, conv_w, n_seq):
    t, d = cg.shape
    s = t // n_seq
    wl = 256
    spec = pl.BlockSpec((s, wl), lambda b, j: (b, j))
    return pl.pallas_call(
        _sc_prompt_kernel,
        out_shape=[jax.ShapeDtypeStruct((t, d), BF16),
                   jax.ShapeDtypeStruct((n_seq, SC_WIDTH - 1, d), F32)],
        grid=(n_seq, d // wl),
        in_specs=[spec, spec, spec, pl.BlockSpec((SC_WIDTH, wl), lambda b, j: (0, j))],
        out_specs=[spec, pl.BlockSpec((1, SC_WIDTH - 1, wl), lambda b, j: (b, 0, j))],
        scratch_shapes=[pltpu.VMEM((s + SUBLANES, wl), F32)],
        compiler_params=_params(2 * 3 * s * wl * 4 + 2 * s * wl * 2 + 6 * s * wl * 4, 2),
        name="short_conv_prompt",
    )(bg, cg, u, conv_w)


def _sc_sample_kernel(b_ref, c_ref, u_ref, st_ref, w_ref, a_ref, sto_ref, *, n_new, d):
    up = [st_ref[:, j * d:(j + 1) * d] for j in range(SC_WIDTH - 1)]
    up += [c_ref[:, t * d:(t + 1) * d].astype(F32) * u_ref[:, t * d:(t + 1) * d].astype(F32) for t in range(n_new)]
    for t in range(n_new):
        conv = up[t] * w_ref[0:1, :]
        for j in range(1, SC_WIDTH):
            conv = conv + up[t + j] * w_ref[j:j + 1, :]
        a_ref[:, t * d:(t + 1) * d] = (b_ref[:, t * d:(t + 1) * d].astype(F32) * conv).astype(a_ref.dtype)
    for j in range(SC_WIDTH - 1):
        sto_ref[:, j * d:(j + 1) * d] = up[n_new + j]


def _sc_sample(bg, cg, u, state, conv_w, n_seq):
    t, d = cg.shape
    n_new = t // n_seq
    wide = lambda x: x.reshape(n_seq, n_new * d)
    a, st = pl.pallas_call(
        functools.partial(_sc_sample_kernel, n_new=n_new, d=d),
        out_shape=[jax.ShapeDtypeStruct((n_seq, n_new * d), BF16),
                   jax.ShapeDtypeStruct((n_seq, (SC_WIDTH - 1) * d), F32)],
        compiler_params=pltpu.CompilerParams(vmem_limit_bytes=_vmem_limit(16 * n_seq * n_new * d * 4)),
        name="short_conv_sample",
    )(wide(bg), wide(cg), wide(u), state.reshape(n_seq, (SC_WIDTH - 1) * d), conv_w)
    return a.reshape(t, d), st.reshape(n_seq, SC_WIDTH - 1, d)


def _post_mix_kernel(*refs, n_in):
    a_refs = refs[:n_in]
    w_refs = refs[n_in:2 * n_in]
    x_ref, gate_ref, g_ref, shift_ref, scale_ref, wr_ref, br_ref, xmid_ref, h_ref, logit_ref = refs[2 * n_in:]
    acc = _dot(a_refs[0][...], w_refs[0][...])
    for a_ref, w_ref in zip(a_refs[1:], w_refs[1:]):
        acc = acc + _dot(a_ref[...], w_ref[...])
    xm = x_ref[...] + _row(gate_ref) * acc
    xmid_ref[...] = xm
    h = _rmsnorm(xm, g_ref[...]) * (1.0 + _row(scale_ref)) + _row(shift_ref)
    h_ref[...] = h
    logit_ref[...] = _dot4(h, wr_ref[...]) + br_ref[...]


def _post_mix(a_list, w_list, x, g, mod, w_route, b_route, name):
    t, d = x.shape
    tm = TOKEN_TILE
    n_in = len(a_list)
    row = lambda w: pl.BlockSpec((tm, w), lambda i: (i, 0))
    const = lambda shp: pl.BlockSpec(shp, lambda i: (0, 0))
    w_bytes = sum(w.size * 2 for w in w_list)
    return pl.pallas_call(
        functools.partial(_post_mix_kernel, n_in=n_in),
        out_shape=[jax.ShapeDtypeStruct((t, d), F32), jax.ShapeDtypeStruct((t, d), F32),
                   jax.ShapeDtypeStruct((t, LANES), F32)],
        grid=(t // tm,),
        in_specs=[row(a.shape[1]) for a in a_list] + [const(w.shape) for w in w_list]
        + [row(d), mod.spec(2, tm), const((1, d)), mod.spec(3, tm), mod.spec(4, tm), const((d, LANES)), const((1, LANES))],
        out_specs=[row(d), row(d), row(LANES)],
        compiler_params=_params(2 * w_bytes + 2 * tm * d * (2 * n_in + 12) + 8 * tm * d * 4, 1),
        name=name,
    )(*a_list, *w_list, x, mod.arr, g.reshape(1, d), mod.arr, mod.arr, w_route, b_route)


def _route_kernel(lg_ref, route_ref, cnt_ref, acc_ref):
    i = pl.program_id(0)
    tm = lg_ref.shape[0]

    @pl.when(i == 0)
    def _():
        acc_ref[...] = jnp.zeros_like(acc_ref)

    lg = lg_ref[...]
    lane = lax.broadcasted_iota(I32, lg.shape, 1)
    neg = -jnp.inf
    gl = jnp.where(lane < N_EXPERT_GROUPS, lg, neg)
    gmax = jnp.max(gl, axis=-1, keepdims=True)
    gidx = jnp.min(jnp.where(gl == gmax, lane, LANES), axis=-1, keepdims=True)
    g_top = 1.0 / jnp.sum(jnp.exp(gl - gmax), axis=-1, keepdims=True)
    lo = ROUTER_LANE0 + gidx * EXPERTS_PER_GROUP
    el = jnp.where((lane >= lo) & (lane < lo + EXPERTS_PER_GROUP), lg, neg)
    m1 = jnp.max(el, axis=-1, keepdims=True)
    i1 = jnp.min(jnp.where(el == m1, lane, LANES), axis=-1, keepdims=True)
    el2 = jnp.where(lane == i1, neg, el)
    m2 = jnp.max(el2, axis=-1, keepdims=True)
    i2 = jnp.min(jnp.where(el2 == m2, lane, LANES), axis=-1, keepdims=True)
    p2 = jnp.exp(m2 - m1)
    w1 = g_top / (1.0 + p2)
    w2 = w1 * p2
    sel1 = lane == i1
    sel2 = lane == i2
    onehot = jnp.where(sel1 | sel2, 1.0, 0.0).astype(BF16)
    r = lax.broadcasted_iota(I32, (tm, tm), 0)
    c = lax.broadcasted_iota(I32, (tm, tm), 1)
    tri = jnp.where(c <= r, 1.0, 0.0).astype(BF16)
    cum = _dot(tri, onehot) + acc_ref[...]
    r1 = jnp.sum(jnp.where(sel1, cum, 0.0), axis=-1, keepdims=True) - 1.0
    r2 = jnp.sum(jnp.where(sel2, cum, 0.0), axis=-1, keepdims=True) - 1.0
    acc_ref[...] = cum[tm - 1:tm, :]
    cnt_ref[...] = cum[tm - 1:tm, :]
    e1 = (i1 - ROUTER_LANE0).astype(F32)
    e2 = (i2 - ROUTER_LANE0).astype(F32)
    rec = jnp.zeros(lg.shape, F32)
    for k, val in enumerate((e1, e2, r1, r2, w1, w2)):
        rec = jnp.where(lane == k, val, rec)
    route_ref[...] = rec


def _route(logits):
    t = logits.shape[0]
    tm = TOKEN_TILE
    return pl.pallas_call(
        _route_kernel,
        out_shape=[jax.ShapeDtypeStruct((t, LANES), F32), jax.ShapeDtypeStruct((1, LANES), F32)],
        grid=(t // tm,),
        in_specs=[pl.BlockSpec((tm, LANES), lambda i: (i, 0))],
        out_specs=[pl.BlockSpec((tm, LANES), lambda i: (i, 0)), pl.BlockSpec((1, LANES), lambda i: (0, 0))],
        scratch_shapes=[pltpu.VMEM((1, LANES), F32)],
        compiler_params=_params(64 * tm * LANES * 4, 1),
        name="moe_route",
    )(logits)


def _row_copy(src_ref, src_row, dst_ref, dst_row, sem):
    return pltpu.make_async_copy(src_ref.at[pl.ds(src_row, 1)], dst_ref.at[pl.ds(dst_row, 1)], sem)


def _dispatch_kernel(dest_ref, h_ref, xs_in_ref, xs_ref, sem):
    del xs_in_ref
    tm = h_ref.shape[0]

    def start(r, _):
        for k in range(TOP_K):
            _row_copy(h_ref, r, xs_ref, dest_ref[k * tm + r], sem).start(priority=k % 2)
        return 0

    lax.fori_loop(0, tm, start, 0, unroll=ROW_DMA_UNROLL)
    for k in range(TOP_K):
        pltpu.make_async_copy(h_ref, xs_ref.at[pl.ds(0, tm)], sem).wait()


def _dispatch(h, dest, xs):
    t, d = h.shape
    tm = min(ROW_DMA_TILE, t)
    return pl.pallas_call(
        _dispatch_kernel,
        out_shape=jax.ShapeDtypeStruct(xs.shape, xs.dtype),
        grid=(t // tm,),
        in_specs=[pl.BlockSpec((TOP_K * tm,), lambda i: (i,), memory_space=pltpu.SMEM),
                  pl.BlockSpec((tm, d), lambda i: (i, 0)),
                  pl.BlockSpec(memory_space=pl.ANY)],
        out_specs=pl.BlockSpec(memory_space=pl.ANY),
        scratch_shapes=[pltpu.SemaphoreType.DMA],
        input_output_aliases={2: 0},
        compiler_params=_params(4 * tm * d * 4, 1),
        name="moe_dispatch",
    )(dest, h, xs)


def _experts_kernel(te_ref, nu_ref, xs_ref, wg_ref, wu_ref, wd_ref, ys_ref, wgu_ref, wdn_ref):
    i = pl.program_id(0)
    f = wg_ref.shape[1]

    @pl.when((i == 0) | (te_ref[i] != te_ref[jnp.maximum(i - 1, 0)]))
    def _():
        wgu_ref[:, :f] = wg_ref[...].astype(BF16)
        wgu_ref[:, f:] = wu_ref[...].astype(BF16)
        wdn_ref[...] = wd_ref[...].astype(BF16)

    @pl.when(i < nu_ref[0])
    def _():
        hgu = _dot(xs_ref[...].astype(BF16), wgu_ref[...])
        hid = (_silu(hgu[:, :f]) * hgu[:, f:]).astype(BF16)
        ys_ref[...] = _dot(hid, wdn_ref[...])

    @pl.when(i >= nu_ref[0])
    def _():
        ys_ref[...] = jnp.zeros_like(ys_ref)


def _experts(xs, tile_expert, n_used, w_gate, w_up, w_down, layer):
    p, d = xs.shape
    tm = MOE_TILE
    f = w_gate.shape[-1]
    row_map = lambda i, te, nu: (jnp.minimum(i, jnp.maximum(nu[0] - 1, 0)), 0)
    return pl.pallas_call(
        _experts_kernel,
        out_shape=jax.ShapeDtypeStruct((p, d), F32),
        grid_spec=pltpu.PrefetchScalarGridSpec(
            num_scalar_prefetch=2,
            grid=(p // tm,),
            in_specs=[pl.BlockSpec((tm, d), row_map),
                      pl.BlockSpec((None, None, d, f), lambda i, te, nu: (layer, te[i], 0, 0)),
                      pl.BlockSpec((None, None, d, f), lambda i, te, nu: (layer, te[i], 0, 0)),
                      pl.BlockSpec((None, None, f, d), lambda i, te, nu: (layer, te[i], 0, 0))],
            out_specs=pl.BlockSpec((tm, d), lambda i, te, nu: (i, 0)),
            scratch_shapes=[pltpu.VMEM((d, 2 * f), BF16), pltpu.VMEM((f, d), BF16)]),
        compiler_params=_params(2 * 3 * d * f * 4 + 4 * tm * d * 4 + 3 * d * f * 2 + 8 * tm * f * 4, 1),
        name="moe_experts",
    )(tile_expert, n_used, xs, w_gate, w_up, w_down)


def _combine_kernel(*refs, final):
    if final:
        dest_ref, route_ref, x_ref, gate_ref, fg_ref, ys_ref, o_ref, y_ref, gbuf_ref, sem = refs
    else:
        dest_ref, route_ref, x_ref, gate_ref, ys_ref, o_ref, gbuf_ref, sem = refs
    tm = x_ref.shape[0]

    def start(r, _):
        for k in range(TOP_K):
            _row_copy(ys_ref, dest_ref[k * tm + r], gbuf_ref.at[k], r, sem).start(priority=k % 2)
        return 0

    lax.fori_loop(0, tm, start, 0, unroll=ROW_DMA_UNROLL)
    for k in range(TOP_K):
        pltpu.make_async_copy(ys_ref.at[pl.ds(0, tm)], gbuf_ref.at[k], sem).wait()
    route = route_ref[...]
    moe = gbuf_ref[0] * route[:, ROUTE_LANE_W:ROUTE_LANE_W + 1]
    for k in range(1, TOP_K):
        moe = moe + gbuf_ref[k] * route[:, ROUTE_LANE_W + k:ROUTE_LANE_W + k + 1]
    x = x_ref[...] + _row(gate_ref) * moe
    o_ref[...] = x
    if final:
        y_ref[...] = _rmsnorm(x, fg_ref[...])


def _combine(dest, route, x_mid, mod, ys, final_g):
    t, d = x_mid.shape
    tm = min(ROW_DMA_TILE, t)
    final = final_g is not None
    row = lambda w: pl.BlockSpec((tm, w), lambda i: (i, 0))
    in_specs = [pl.BlockSpec((TOP_K * tm,), lambda i: (i,), memory_space=pltpu.SMEM),
                row(LANES), row(d), mod.spec(5, tm)]
    args = [dest, route, x_mid, mod.arr]
    if final:
        in_specs.append(pl.BlockSpec((1, d), lambda i: (0, 0)))
        args.append(final_g.reshape(1, d))
    in_specs.append(pl.BlockSpec(memory_space=pl.ANY))
    args.append(ys)
    out_shape = [jax.ShapeDtypeStruct((t, d), F32)] * (2 if final else 1)
    out_specs = [row(d)] * (2 if final else 1)
    return pl.pallas_call(
        functools.partial(_combine_kernel, final=final),
        out_shape=out_shape,
        grid=(t // tm,),
        in_specs=in_specs,
        out_specs=out_specs,
        scratch_shapes=[pltpu.VMEM((TOP_K, tm, d), F32), pltpu.SemaphoreType.DMA],
        compiler_params=_params(8 * tm * d * 4 + TOP_K * tm * d * 4, 1),
        name="moe_combine",
    )(*args)


def _moe(layer, h_p, h_s, logit_p, logit_s, xmid_p, xmid_s, mod_p, mod_s, w_gate, w_up, w_down, final_g):
    tp, d = h_p.shape
    ts = h_s.shape[0]
    route, counts = _route(jnp.concatenate([logit_p, logit_s], axis=0))
    cnt = counts[0, ROUTER_LANE0:ROUTER_LANE0 + N_EXPERTS].astype(I32)
    padded = (cnt + MOE_TILE - 1) // MOE_TILE * MOE_TILE
    ends = jnp.cumsum(padded)
    starts = ends - padded
    rec = route[:, :SUBLANES].T
    expert = rec[ROUTE_LANE_E:ROUTE_LANE_E + TOP_K].astype(I32)
    rank = rec[ROUTE_LANE_R:ROUTE_LANE_R + TOP_K].astype(I32)
    is_e = expert[None] == jnp.arange(N_EXPERTS, dtype=I32)[:, None, None]
    dest = jnp.sum(jnp.where(is_e, starts[:, None, None], 0), axis=0) + rank

    def by_tile(a):
        tm = min(ROW_DMA_TILE, a.shape[1])
        return jnp.transpose(a.reshape(TOP_K, a.shape[1] // tm, tm), (1, 0, 2)).reshape(-1)

    dest_p, dest_s = by_tile(dest[:, :tp]), by_tile(dest[:, tp:])
    n_rows = (tp + ts) * TOP_K + N_EXPERTS * MOE_TILE
    n_tiles = n_rows // MOE_TILE
    n_used = (ends[-1:] // MOE_TILE).astype(I32)
    tile_start = jnp.arange(n_tiles, dtype=I32) * MOE_TILE
    tile_expert = jnp.minimum(jnp.sum((ends[None, :] <= tile_start[:, None]).astype(I32), axis=1), N_EXPERTS - 1)
    xs = jnp.zeros((n_rows, d), F32)
    xs = _dispatch(h_p, dest_p, xs)
    xs = _dispatch(h_s, dest_s, xs)
    ys = _experts(xs, tile_expert, n_used, w_gate, w_up, w_down, layer)
    out_p = _combine(dest_p, route[:tp], xmid_p, mod_p, ys, final_g)
    out_s = _combine(dest_s, route[tp:], xmid_s, mod_s, ys, final_g)
    return out_p, out_s


def kernel(x_prompt, x_sample, c_prompt, c_sample, state_ssm, state_ssd_conv, cache_k, cache_v, page_table, state_short_conv, ada_w, ada_b, ln_mix_g, ln_ffn_g, mix_w_in, mix_w_out, ssd_conv_w, ssd_conv_b, ssd_dt_bias, ssd_a_log, ssd_d, ssd_norm_g, sb_bias, sc_w_in, sc_conv_w, sc_w_out, moe_w_group, moe_b_group, moe_w_router, moe_b_router, moe_w_gate, moe_w_up, moe_w_down, final_g):
    b, s, d = x_prompt.shape
    bs, ls, _ = x_sample.shape
    tp, ts = b * s, bs * ls
    depth = ada_w.shape[0]
    xp = x_prompt.reshape(tp, d)
    xs = x_sample.reshape(ts, d)
    mod = _adaln(jnp.concatenate([c_prompt, c_sample], axis=0), ada_w, ada_b)
    ls_pad = SUBLANES

    def pad_rows(a, rows, front=0):
        return jnp.pad(a, ((0, 0), (front, rows - a.shape[1] - front), (0, 0)))

    ssm_p, ssm_s, cv_p, cv_s, k_p, k_s, v_p, v_s, sc_p, sc_s = ([] for _ in range(10))
    y_out = None
    for layer in range(depth):
        i = layer // 2
        mod_p = _Mod(mod[layer, :b].reshape(b, 1, 6 * d), s, d)
        mod_s = _Mod(jnp.repeat(mod[layer, b:], ls, axis=0), ls, d)
        n_route = N_EXPERT_GROUPS + N_EXPERTS
        w_route = jnp.pad(jnp.concatenate([moe_w_group[layer], moe_w_router[layer]], axis=1),
                          ((0, 0), (0, LANES - n_route)))
        b_route = jnp.pad(jnp.concatenate([moe_b_group[layer], moe_b_router[layer]]),
                          (0, LANES - n_route)).reshape(1, LANES)
        if layer % 2 == 0:
            w = mix_w_in[i]
            o_dt = D_SSM + CONV_DIM
            o_q = o_dt + SSM_HEADS
            w_bf = jnp.concatenate(
                [w[:, :o_dt], w[:, o_q:o_q + D_ATT] * SB_SCALE, w[:, o_q + D_ATT:], w[:, o_dt:o_q],
                 jnp.zeros((d, LANES - SSM_HEADS), F32)], axis=1).astype(BF16)
            c0 = D_SSM + CONV_DIM
            splits = lambda kv, zt: ((0, D_SSM, (zt,)), (D_SSM, CONV_DIM, (F32,)), (c0, D_ATT, (BF16,)),
                                     (c0 + D_ATT, D_ATT, (kv, BF16)), (c0 + 2 * D_ATT, D_ATT, (kv, BF16)),
                                     (c0 + 3 * D_ATT, LANES, (F32,)))
            z1, xbc1, q1, kt1, kb1, vt1, vb1, dt1 = _pre_mix(
                xp, ln_mix_g[layer], mod_p, w_bf, splits(POSITION_LAST, BF16), "pre_mix_even_p")
            z2, xbc2, q2, kf2, kb2, vf2, vb2, dt2 = _pre_mix(
                xs, ln_mix_g[layer], mod_s, w_bf, splits(F32, F32), "pre_mix_even_s")
            prm = (ssd_conv_w[i], ssd_conv_b[i].reshape(1, CONV_DIM),
                   jnp.pad(ssd_dt_bias[i], (0, LANES - SSM_HEADS)).reshape(1, LANES),
                   jnp.pad(ssd_a_log[i], (0, LANES - SSM_HEADS)).reshape(1, LANES),
                   jnp.repeat(ssd_d[i], SSM_HEAD_DIM).reshape(1, D_SSM),
                   ssd_norm_g[i].reshape(1, D_SSM))
            y1, hfin1, cv1 = _ssd(xbc1, z1, dt1, prm, b, SSD_CHUNK, SSD_CHUNK, name="ssd_prompt")
            seq_pad = lambda a: pad_rows(a.reshape(bs, ls, a.shape[-1]), ls_pad).reshape(bs * ls_pad, a.shape[-1])
            y2, hfin2, cv2 = _ssd(
                seq_pad(xbc2), seq_pad(z2), seq_pad(dt2), prm, bs, ls_pad, ls,
                tail0=pad_rows(state_ssd_conv[i], SUBLANES, front=SUBLANES - (SSD_CONV - 1)),
                h0=state_ssm[i].reshape(bs, D_SSM, D_STATE), name="ssd_sample")
            y2 = y2.reshape(bs, ls_pad, D_SSM)[:, :ls].reshape(ts, D_SSM)
            att1 = _attn_prompt(q1, kb1, vb1, sb_bias[i], b)
            pos_last = lambda a: jnp.transpose(a, (0, 1, 3, 4, 2))
            att2 = _attn_sample(
                q2.reshape(bs, ls, D_ATT), pad_rows(kf2.reshape(bs, ls, D_ATT), SUBLANES),
                pad_rows(vf2.reshape(bs, ls, D_ATT), SUBLANES),
                pos_last(cache_k), pos_last(cache_v), i, page_table, sb_bias[i]).reshape(ts, D_ATT)
            w_out = mix_w_out[i].astype(BF16)
            w_list = [w_out[:D_SSM], w_out[D_SSM:]]
            xm1, hf1, lg1 = _post_mix([y1, att1], w_list, xp, ln_ffn_g[layer], mod_p, w_route, b_route, "post_mix_even_p")
            xm2, hf2, lg2 = _post_mix([y2, att2], w_list, xs, ln_ffn_g[layer], mod_s, w_route, b_route, "post_mix_even_s")
            ssm_p.append(hfin1.reshape(b, SSM_HEADS, SSM_HEAD_DIM, D_STATE))
            ssm_s.append(hfin2.reshape(bs, SSM_HEADS, SSM_HEAD_DIM, D_STATE))
            cv_p.append(cv1)
            cv_s.append(cv2)
            seq_first = lambda a: jnp.transpose(a.reshape(b, ATT_HEADS, ATT_HEAD_DIM, s), (0, 3, 1, 2))
            k_p.append(seq_first(kt1))
            k_s.append(kf2.reshape(bs, ls, ATT_HEADS, ATT_HEAD_DIM))
            v_p.append(seq_first(vt1))
            v_s.append(vf2.reshape(bs, ls, ATT_HEADS, ATT_HEAD_DIM))
        else:
            w_bf = sc_w_in[i].astype(BF16)
            splits = tuple((j * d, d, (BF16,)) for j in range(3))
            bg1, cg1, u1 = _pre_mix(xp, ln_mix_g[layer], mod_p, w_bf, splits, "pre_mix_odd_p")
            bg2, cg2, u2 = _pre_mix(xs, ln_mix_g[layer], mod_s, w_bf, splits, "pre_mix_odd_s")
            a1, st1 = _sc_prompt(bg1, cg1, u1, sc_conv_w[i], b)
            a2, st2 = _sc_sample(bg2, cg2, u2, state_short_conv[i], sc_conv_w[i], bs)
            w_list = [sc_w_out[i].astype(BF16)]
            xm1, hf1, lg1 = _post_mix([a1], w_list, xp, ln_ffn_g[layer], mod_p, w_route, b_route, "post_mix_odd_p")
            xm2, hf2, lg2 = _post_mix([a2], w_list, xs, ln_ffn_g[layer], mod_s, w_route, b_route, "post_mix_odd_s")
            sc_p.append(st1)
            sc_s.append(st2)
        fg = final_g if layer == depth - 1 else None
        out_p, out_s = _moe(layer, hf1, hf2, lg1, lg2, xm1, xm2, mod_p, mod_s,
                            moe_w_gate, moe_w_up, moe_w_down, fg)
        xp, xs = out_p[0], out_s[0]
        if fg is not None:
            y_out = (out_p[1].reshape(b, s, d), out_s[1].reshape(bs, ls, d))
    return (y_out[0], y_out[1], jnp.stack(ssm_p), jnp.stack(ssm_s), jnp.stack(cv_p), jnp.stack(cv_s),
            jnp.stack(k_p), jnp.stack(k_s), jnp.stack(v_p), jnp.stack(v_s), jnp.stack(sc_p), jnp.stack(sc_s))
```

```python
import functools

import jax
import jax.numpy as jnp
from jax import lax
from jax.experimental import pallas as pl
from jax.experimental.pallas import tpu as pltpu

F32 = jnp.float32
BF16 = jnp.bfloat16
I32 = jnp.int32

SSM_HEADS = 16
SSM_HEAD_DIM = 64
D_SSM = SSM_HEADS * SSM_HEAD_DIM
SSM_GROUPS = 2
HEADS_PER_GROUP = SSM_HEADS // SSM_GROUPS
D_STATE = 128
SSD_CONV = 4
SSD_CHUNK = 128
CONV_DIM = D_SSM + 2 * SSM_GROUPS * D_STATE
ATT_HEADS = 16
ATT_HEAD_DIM = 64
D_ATT = ATT_HEADS * ATT_HEAD_DIM
SB_SCALE = ATT_HEAD_DIM ** -0.5
SC_WIDTH = 3
N_EXPERT_GROUPS = 4
EXPERTS_PER_GROUP = 8
N_EXPERTS = N_EXPERT_GROUPS * EXPERTS_PER_GROUP
TOP_K = 2
D_EXPERT = 256
RMS_EPS = 1e-6
NEG_LOG2_E = -1.4426950408889634

LANES = 128
SUBLANES = 8
VMEM_BYTES_V7X = 64 * 1024 * 1024
VMEM_LIMIT_CAP = VMEM_BYTES_V7X - 8 * 1024 * 1024

TOKEN_TILE = 256
ATT_TILE = 256
ATT_HEAD_BLOCK = 4
MOE_TILE = 256
ROW_DMA_TILE = 512
ROW_DMA_UNROLL = 8
ROUTE_LANE_E = 0
ROUTE_LANE_R = 2
ROUTE_LANE_W = 4
ROUTER_LANE0 = N_EXPERT_GROUPS
POSITION_LAST = "f32, position axis last"


def _vmem_limit(nbytes):
    return int(min(max(2 * nbytes, 32 * 1024 * 1024), VMEM_LIMIT_CAP))


def _params(nbytes, n_axes):
    return pltpu.CompilerParams(dimension_semantics=("arbitrary",) * n_axes,
                                vmem_limit_bytes=_vmem_limit(nbytes))


def _dot(a, b):
    return jnp.dot(a, b, preferred_element_type=F32)


def _dot_nt(a, b):
    return lax.dot_general(a, b, (((1,), (1,)), ((), ())), preferred_element_type=F32)


def _dot_tn(a, b):
    return lax.dot_general(a, b, (((0,), (0,)), ((), ())), preferred_element_type=F32)


def _split2(x):
    hi = x.astype(BF16)
    lo = (x - hi.astype(F32)).astype(BF16)
    return hi, lo


def _split3(x):
    hi = x.astype(BF16)
    r = x - hi.astype(F32)
    mid = r.astype(BF16)
    lo = (r - mid.astype(F32)).astype(BF16)
    return hi, mid, lo


def _dot3(a, b):
    ah, al = _split2(a)
    bh, bl = _split2(b)
    return _dot(ah, bh) + (_dot(al, bh) + _dot(ah, bl))


def _dot4(a, b):
    m, n = a.shape[0], b.shape[1]
    r = _dot(jnp.concatenate(_split2(a), axis=0), jnp.concatenate(_split2(b), axis=1))
    return (r[:m, :n] + r[m:, n:]) + (r[:m, n:] + r[m:, :n])


def _silu(x):
    return x / (1.0 + jnp.exp(-x))


def _softplus(x):
    return jnp.maximum(x, 0.0) + jnp.log(1.0 + jnp.exp2(jnp.abs(x) * NEG_LOG2_E))


def _rmsnorm(x, g):
    return x * lax.rsqrt(jnp.mean(x * x, axis=-1, keepdims=True) + RMS_EPS) * g


def _row(ref):
    v = ref[...]
    return v[0] if v.ndim == 3 else v


def _adaln_kernel(c_ref, w_ref, b_ref, o_ref):
    o_ref[...] = _dot3(_silu(c_ref[...]), w_ref[...]) + b_ref[...]


def _adaln(c_all, ada_w, ada_b):
    depth, d, n = ada_w.shape
    rows = c_all.shape[0]
    tn = 1024
    return pl.pallas_call(
        _adaln_kernel,
        out_shape=jax.ShapeDtypeStruct((depth, rows, n), F32),
        grid=(depth, n // tn),
        in_specs=[pl.BlockSpec((rows, d), lambda l, j: (0, 0)),
                  pl.BlockSpec((None, d, tn), lambda l, j: (l, 0, j)),
                  pl.BlockSpec((None, 1, tn), lambda l, j: (l, 0, j))],
        out_specs=pl.BlockSpec((None, rows, tn), lambda l, j: (l, 0, j)),
        compiler_params=_params(2 * d * tn * 4 + 4 * rows * (d + tn) * 4, 2),
        name="adaln",
    )(c_all, ada_w, ada_b.reshape(depth, 1, n))


class _Mod:
    def __init__(self, arr, seq_len, d):
        self.arr = arr
        self.seq_len = seq_len
        self.d = d

    def spec(self, which, tm):
        d = self.d
        if self.arr.ndim == 3:
            tiles_per_seq = self.seq_len // tm
            return pl.BlockSpec((1, 1, d), lambda i, *_: (i // tiles_per_seq, 0, which))
        return pl.BlockSpec((tm, d), lambda i, *_: (i, which))


def _pre_mix_kernel(x_ref, g_ref, shift_ref, scale_ref, w_ref, *out_refs, splits):
    h = (_rmsnorm(x_ref[...], g_ref[...]) * (1.0 + _row(scale_ref)) + _row(shift_ref)).astype(BF16)
    k = 0
    for start, width, dtypes in splits:
        acc = _dot(h, w_ref[:, start:start + width])
        for dt in dtypes:
            if dt == POSITION_LAST:
                out_refs[k][...] = acc.T
            else:
                out_refs[k][...] = acc.astype(dt)
            k += 1


def _pre_mix(x, g, mod, w_bf, splits, name):
    t, d = x.shape
    n = w_bf.shape[1]
    tm = TOKEN_TILE
    tiles_per_seq = mod.seq_len // tm
    out_shape, out_specs, out_bytes = [], [], 0
    for _, width, dtypes in splits:
        for dt in dtypes:
            if dt == POSITION_LAST:
                out_shape.append(jax.ShapeDtypeStruct((t // mod.seq_len, width, mod.seq_len), F32))
                out_specs.append(pl.BlockSpec((None, width, tm), lambda i: (i // tiles_per_seq, 0, i % tiles_per_seq)))
                out_bytes += tm * width * 4
            else:
                out_shape.append(jax.ShapeDtypeStruct((t, width), dt))
                out_specs.append(pl.BlockSpec((tm, width), lambda i: (i, 0)))
                out_bytes += tm * width * jnp.dtype(dt).itemsize
    return pl.pallas_call(
        functools.partial(_pre_mix_kernel, splits=splits),
        out_shape=out_shape,
        grid=(t // tm,),
        in_specs=[pl.BlockSpec((tm, d), lambda i: (i, 0)),
                  pl.BlockSpec((1, d), lambda i: (0, 0)),
                  mod.spec(0, tm), mod.spec(1, tm),
                  pl.BlockSpec((d, n), lambda i: (0, 0))],
        out_specs=out_specs,
        compiler_params=_params(2 * d * n * 2 + 2 * tm * d * 4 + 2 * out_bytes + tm * n * 4, 1),
        name=name,
    )(x, g.reshape(1, d), mod.arr, mod.arr, w_bf)


def _ssd_kernel(*refs, chunk, n_real, has_init):
    if has_init:
        (xbc_ref, z_ref, dt_ref, tail0_ref, h0_ref, cw_ref, cb_ref, dtb_ref, alog_ref, dsk_ref, ng_ref,
         y_ref, hout_ref, cvout_ref, buf_ref, state_ref, ybuf_ref) = refs
    else:
        (xbc_ref, z_ref, dt_ref, cw_ref, cb_ref, dtb_ref, alog_ref, dsk_ref, ng_ref,
         y_ref, hout_ref, cvout_ref, buf_ref, state_ref, ybuf_ref) = refs
    c = pl.program_id(1)
    last = pl.num_programs(1) - 1
    L = chunk
    T0 = SUBLANES

    @pl.when(c == 0)
    def _():
        if has_init:
            buf_ref[0:T0, :] = tail0_ref[0]
            state_ref[...] = h0_ref[0]
        else:
            buf_ref[0:T0, :] = jnp.zeros((T0, CONV_DIM), F32)
            state_ref[...] = jnp.zeros_like(state_ref)

    xbc = xbc_ref[...]
    buf_ref[T0:T0 + L, :] = xbc
    x_ext = buf_ref[0:T0 + L, :]
    conv = xbc * cw_ref[SSD_CONV - 1:SSD_CONV, :]
    for j in range(SSD_CONV - 1):
        shifted = pltpu.roll(x_ext, SSD_CONV - 1 - j, axis=0)[T0:]
        conv = conv + shifted * cw_ref[j:j + 1, :]
    xc = _silu(conv + cb_ref[...])
    xs = xc[:, :D_SSM]
    bmat = xc[:, D_SSM:D_SSM + SSM_GROUPS * D_STATE].astype(BF16)
    cmat = xc[:, D_SSM + SSM_GROUPS * D_STATE:].astype(BF16)

    @pl.when(c == last)
    def _():
        cvout_ref[0] = buf_ref[pl.ds(T0 + n_real - (SSD_CONV - 1), SSD_CONV - 1), :]

    if n_real == L:
        buf_ref[0:T0, :] = buf_ref[L:L + T0, :]

    dt = _softplus(dt_ref[...] + dtb_ref[...])
    rows = lax.broadcasted_iota(I32, (L, L), 0)
    cols = lax.broadcasted_iota(I32, (L, L), 1)
    if n_real < L:
        dt = jnp.where(lax.broadcasted_iota(I32, dt.shape, 0) < n_real, dt, 0.0)
    a = -jnp.exp(alog_ref[...])
    tri = cols <= rows
    tri_bf = jnp.where(tri, 1.0, 0.0).astype(BF16)
    da_hi, da_mid, da_lo = _split3(dt * a)
    acum = _dot(tri_bf, da_hi) + (_dot(tri_bf, da_mid) + _dot(tri_bf, da_lo))
    acum_t = acum.T
    a_last = acum[L - 1:L, :]
    chunk_decay = jnp.exp(a_last)
    per_head = jnp.concatenate([dt, jnp.exp(acum), jnp.exp(a_last - acum)], axis=0)
    head_of_lane = lax.broadcasted_iota(I32, (3 * LANES, D_SSM), 1) // SSM_HEAD_DIM
    expand3 = jnp.where(lax.broadcasted_iota(I32, (3 * LANES, D_SSM), 0) % LANES == head_of_lane, 1.0, 0.0).astype(BF16)
    wide = _dot(jnp.concatenate(_split3(per_head), axis=1), expand3)
    dt_w, e_acum_w, to_end_w = wide[:L], wide[L:2 * L], wide[2 * L:]
    xdt = xs * dt_w
    xdt_bf = xdt.astype(BF16)
    x_end = (xdt * to_end_w).astype(BF16)

    gw = HEADS_PER_GROUP * SSM_HEAD_DIM
    for g in range(SSM_GROUPS):
        gl = slice(g * gw, (g + 1) * gw)
        gs = slice(g * D_STATE, (g + 1) * D_STATE)
        st = state_ref[gl, :]
        ybuf_ref[:, gl] = _dot_nt(cmat[:, gs], st.astype(BF16)) * e_acum_w[:, gl] + xs[:, gl] * dsk_ref[:, gl]
        s_new = _dot_tn(x_end[:, gl], bmat[:, gs])
        for h in range(g * HEADS_PER_GROUP, (g + 1) * HEADS_PER_GROUP):
            hs = slice(h * SSM_HEAD_DIM, (h + 1) * SSM_HEAD_DIM)
            hg = slice(hs.start - gl.start, hs.stop - gl.start)
            state_ref[hs, :] = st[hg] * chunk_decay[:, h:h + 1] + s_new[hg]

    cb = [_dot_nt(cmat[:, g * D_STATE:(g + 1) * D_STATE], bmat[:, g * D_STATE:(g + 1) * D_STATE])
          for g in range(SSM_GROUPS)]
    first_half = lax.broadcasted_iota(I32, (1, LANES), 1) < SSM_HEAD_DIM
    for pair in range(SSM_HEADS // 2):
        g = (2 * pair) // HEADS_PER_GROUP
        pl_ = slice(pair * LANES, (pair + 1) * LANES)
        ms = []
        for h in (2 * pair, 2 * pair + 1):
            seg = acum[:, h:h + 1] - acum_t[h:h + 1, :]
            ms.append((cb[g] * jnp.exp(jnp.where(tri, seg, -jnp.inf))).astype(BF16))
        xp = xdt_bf[:, pl_]
        zero = jnp.zeros_like(xp)
        x_bd = jnp.concatenate([jnp.where(first_half, xp, zero), jnp.where(first_half, zero, xp)], axis=0)
        ybuf_ref[:, pl_] += _dot(jnp.concatenate(ms, axis=1), x_bd)

    y = ybuf_ref[...] * _silu(z_ref[...].astype(F32))
    half = D_SSM // SSM_GROUPS
    for g in range(SSM_GROUPS):
        yg = y[:, g * half:(g + 1) * half]
        y_ref[:, g * half:(g + 1) * half] = (
            yg * lax.rsqrt(jnp.mean(yg * yg, axis=-1, keepdims=True) + RMS_EPS) * ng_ref[:, g * half:(g + 1) * half]
        ).astype(y_ref.dtype)

    @pl.when(c == last)
    def _():
        hout_ref[0] = state_ref[...]


def _ssd(xbc, z, dt, prm, n_seq, chunk, n_real, tail0=None, h0=None, name="ssd"):
    t = xbc.shape[0]
    nc = t // (n_seq * chunk)
    has_init = tail0 is not None
    cw, cb, dtb, alog, dsk, ng = prm
    row_spec = lambda w: pl.BlockSpec((chunk, w), lambda b, c: (b * nc + c, 0))
    const = lambda shp: pl.BlockSpec(shp, lambda b, c: (0,) * len(shp))
    in_specs = [row_spec(CONV_DIM), row_spec(D_SSM), row_spec(LANES)]
    args = [xbc, z, dt]
    if has_init:
        in_specs += [pl.BlockSpec((1, SUBLANES, CONV_DIM), lambda b, c: (b, 0, 0)),
                     pl.BlockSpec((1, D_SSM, D_STATE), lambda b, c: (b, 0, 0))]
        args += [tail0, h0]
    in_specs += [const((SSD_CONV, CONV_DIM)), const((1, CONV_DIM)), const((1, LANES)), const((1, LANES)),
                 const((1, D_SSM)), const((1, D_SSM))]
    args += [cw, cb, dtb, alog, dsk, ng]
    scratch_bytes = ((chunk + 2 * SUBLANES) * CONV_DIM + D_SSM * D_STATE + chunk * D_SSM) * 4
    block_bytes = chunk * (CONV_DIM + D_SSM + LANES) * 4 + chunk * D_SSM * 2 + 2 * D_SSM * D_STATE * 4
    return pl.pallas_call(
        functools.partial(_ssd_kernel, chunk=chunk, n_real=n_real, has_init=has_init),
        out_shape=[jax.ShapeDtypeStruct((t, D_SSM), BF16),
                   jax.ShapeDtypeStruct((n_seq, D_SSM, D_STATE), F32),
                   jax.ShapeDtypeStruct((n_seq, SSD_CONV - 1, CONV_DIM), F32)],
        grid=(n_seq, nc),
        in_specs=in_specs,
        out_specs=[row_spec(D_SSM),
                   pl.BlockSpec((1, D_SSM, D_STATE), lambda b, c: (b, 0, 0)),
                   pl.BlockSpec((1, SSD_CONV - 1, CONV_DIM), lambda b, c: (b, 0, 0))],
        scratch_shapes=[pltpu.VMEM((chunk + 2 * SUBLANES, CONV_DIM), F32),
                        pltpu.VMEM((D_SSM, D_STATE), F32),
                        pltpu.VMEM((chunk, D_SSM), F32)],
        compiler_params=_params(2 * block_bytes + scratch_bytes + 8 * chunk * CONV_DIM * 4, 2),
        name=name,
    )(*args)


MASKED_LOGIT = -1e30


def _sb_scores(s, neg_upper):
    sp = _softplus(s)
    if neg_upper.shape[0] == 2 * s.shape[1]:
        later = _dot(jnp.concatenate(_split2(sp), axis=1), neg_upper)
        return s - sp, later, later[:, 0:1] - sp[:, 0:1]
    later = _dot(sp.astype(BF16), neg_upper)
    return s - sp, later, -jnp.sum(sp, axis=-1, keepdims=True)


def _neg_strict_upper(n, copies):
    j = lax.broadcasted_iota(I32, (copies * n, n), 0) % n
    k = lax.broadcasted_iota(I32, (copies * n, n), 1)
    return jnp.where(j > k, -1.0, 0.0).astype(BF16)


def _attn_prompt_kernel(bias_ref, q_ref, k_ref, v_ref, o_ref, qs_ref, carry_ref, acc_ref, s_ref, wp_ref):
    hq = pl.program_id(1)
    qi = pl.program_id(2)
    tq, width = q_ref.shape
    tk = ATT_TILE
    nh = ATT_HEAD_BLOCK
    m = nh * tq
    lane_head = lax.broadcasted_iota(I32, (1, width), 1) // ATT_HEAD_DIM
    q = q_ref[...]
    for h in range(nh):
        qs_ref[h * tq:(h + 1) * tq, :] = jnp.where(lane_head == h, q, jnp.zeros_like(q))
    row_head = lax.broadcasted_iota(I32, (m, 1), 0) // tq
    bias = jnp.zeros((m, 1), F32)
    for h in range(nh):
        bias = jnp.where(row_head == h, bias_ref[hq * nh + h], bias)
    neg_upper = _neg_strict_upper(tk, 1)
    causal = lax.broadcasted_iota(I32, (m, tk), 1) < lax.broadcasted_iota(I32, (m, tk), 0) % tq

    def scores(j):
        return _dot_nt(qs_ref[...], k_ref[pl.ds(pl.multiple_of(j * tk, tk), tk), :])

    def tile(j, mask):
        vt = v_ref[pl.ds(pl.multiple_of(j * tk, tk), tk), :]
        s = s_ref[...] + bias
        s_ref[...] = scores(jnp.maximum(j - 1, 0))
        if mask is not None:
            s = jnp.where(mask, s, MASKED_LOGIT)
        log_beta, later, total = _sb_scores(s, neg_upper)
        carry = carry_ref[...]
        w = jnp.exp(log_beta + (later + carry)).astype(BF16)
        carry_ref[...] = carry + total
        del vt
        pv(jnp.minimum(j + 1, qi))
        wp_ref[...] = jnp.concatenate([w[h * tq:(h + 1) * tq] for h in range(nh)], axis=1)

    def pv(j):
        vt = v_ref[pl.ds(pl.multiple_of(j * tk, tk), tk), :]
        v_cat = jnp.concatenate([jnp.where(lane_head == h, vt, jnp.zeros_like(vt)) for h in range(nh)], axis=0)
        acc_ref[...] += _dot(wp_ref[...], v_cat)

    carry_ref[...] = jnp.zeros_like(carry_ref)
    acc_ref[...] = jnp.zeros_like(acc_ref)
    wp_ref[...] = jnp.zeros_like(wp_ref)
    s_ref[...] = scores(qi)
    tile(qi, causal)

    def body(i, c):
        tile(qi - 1 - i, None)
        return c

    lax.fori_loop(0, qi, body, 0)
    pv(0)
    o_ref[...] = acc_ref[...].astype(o_ref.dtype)


def _attn_prompt(q, k, v, bias, n_seq):
    t, d = q.shape
    s = t // n_seq
    tq = ATT_TILE
    nq = s // tq
    width = ATT_HEAD_BLOCK * ATT_HEAD_DIM
    return pl.pallas_call(
        _attn_prompt_kernel,
        out_shape=jax.ShapeDtypeStruct((t, d), BF16),
        grid_spec=pltpu.PrefetchScalarGridSpec(
            num_scalar_prefetch=1,
            grid=(n_seq, d // width, nq),
            in_specs=[pl.BlockSpec((tq, width), lambda b, hq, qi, *_: (b * nq + qi, hq)),
                      pl.BlockSpec((s, width), lambda b, hq, qi, *_: (b, hq)),
                      pl.BlockSpec((s, width), lambda b, hq, qi, *_: (b, hq))],
            out_specs=pl.BlockSpec((tq, width), lambda b, hq, qi, *_: (b * nq + qi, hq)),
            scratch_shapes=[pltpu.VMEM((ATT_HEAD_BLOCK * tq, width), BF16),
                            pltpu.VMEM((ATT_HEAD_BLOCK * tq, 1), F32),
                            pltpu.VMEM((tq, width), F32),
                            pltpu.VMEM((ATT_HEAD_BLOCK * tq, ATT_TILE), F32),
                            pltpu.VMEM((tq, ATT_HEAD_BLOCK * ATT_TILE), BF16)]),
        compiler_params=_params(4 * s * width * 2 + 4 * tq * width * 2
                                + 12 * ATT_HEAD_BLOCK * tq * ATT_TILE * 4, 3),
        name="sb_attn_prompt",
    )(bias, q, k, v)


def _attn_sample_kernel(pt_ref, bias_ref, q_ref, kn_ref, vn_ref, *refs, n_pages, n_new):
    k_refs = refs[:n_pages]
    v_refs = refs[n_pages:2 * n_pages]
    o_ref = refs[2 * n_pages]
    d = q_ref.shape[-1]
    page = k_refs[0].shape[-1]
    n_rows = n_new * ATT_HEADS
    n_blocks = n_pages + 1
    m = n_blocks * n_rows
    row_head = lax.broadcasted_iota(I32, (n_rows, 1), 0) % ATT_HEADS
    lane_head = lax.broadcasted_iota(I32, (1, d), 1) // ATT_HEAD_DIM
    own = lane_head == row_head
    q = q_ref[0]
    qx = jnp.broadcast_to(q[:, None, :], (n_new, ATT_HEADS, d)).reshape(n_rows, d)
    qx = jnp.where(own, qx, jnp.zeros_like(qx))
    bias = jnp.zeros((n_rows, 1), F32)
    for h in range(ATT_HEADS):
        bias = jnp.where(row_head == h, bias_ref[h], bias)

    pad = jnp.zeros((page - kn_ref.shape[1], d), F32)
    k_new = jnp.concatenate([kn_ref[0], pad], axis=0).astype(BF16)
    v_new = jnp.concatenate([vn_ref[0], pad], axis=0).astype(BF16)
    order = range(n_pages - 1, -1, -1)
    kt = jnp.concatenate([k_refs[p][...].reshape(d, page).astype(BF16) for p in order], axis=1)
    vt = jnp.concatenate([v_refs[p][...].reshape(d, page).astype(BF16) for p in order], axis=1)

    s_pages = _dot(qx, kt)
    s = jnp.concatenate([_dot_nt(qx, k_new)] + [s_pages[:, i * page:(i + 1) * page] for i in range(n_pages)], axis=0)
    s = s + jnp.concatenate([bias] * n_blocks, axis=0)
    row = lax.broadcasted_iota(I32, (m, page), 0)
    col = lax.broadcasted_iota(I32, (m, page), 1)
    s = jnp.where((row >= n_rows) | (col < row // ATT_HEADS), s, MASKED_LOGIT)
    log_beta, later, total = _sb_scores(s, _neg_strict_upper(page, 2))
    carry = jnp.zeros((n_rows, 1), F32)
    carries = []
    for i in range(n_blocks):
        carries.append(carry)
        carry = carry + total[i * n_rows:(i + 1) * n_rows]
    w = jnp.exp(log_beta + (later + jnp.concatenate(carries, axis=0))).astype(BF16)
    w_pages = jnp.concatenate([w[(i + 1) * n_rows:(i + 2) * n_rows] for i in range(n_pages)], axis=1)
    acc = _dot(w[:n_rows], v_new) + _dot_nt(w_pages, vt)
    acc = jnp.where(own, acc, 0.0)
    o_ref[0] = jnp.sum(acc.reshape(n_new, ATT_HEADS, d), axis=1).astype(o_ref.dtype)


def _attn_sample(q, k_new, v_new, cache_kt, cache_vt, layer, page_table, bias):
    bs, n_new, d = q.shape
    n_pages = page_table.shape[1]
    _, _, n_heads, dh, page = cache_kt.shape
    page_spec = lambda p: pl.BlockSpec((None, None, n_heads, dh, page),
                                       lambda b, pt, bias_: (layer, pt[b, p], 0, 0, 0))
    seq_spec = lambda rows: pl.BlockSpec((1, rows, d), lambda b, pt, bias_: (b, 0, 0))
    page_bytes = page * d * 4
    return pl.pallas_call(
        functools.partial(_attn_sample_kernel, n_pages=n_pages, n_new=n_new),
        out_shape=jax.ShapeDtypeStruct((bs, n_new, d), BF16),
        grid_spec=pltpu.PrefetchScalarGridSpec(
            num_scalar_prefetch=2,
            grid=(bs,),
            in_specs=[seq_spec(n_new), seq_spec(k_new.shape[1]), seq_spec(v_new.shape[1])]
            + [page_spec(p) for p in range(n_pages)] * 2,
            out_specs=seq_spec(n_new)),
        compiler_params=_params(2 * 2 * n_pages * page_bytes + 8 * page_bytes, 1),
        name="sb_attn_sample",
    )(page_table, bias, q, k_new, v_new, *([cache_kt] * n_pages), *([cache_vt] * n_pages))


def _sc_prompt_kernel(b_ref, c_ref, u_ref, w_ref, a_ref, st_ref, buf_ref):
    L = c_ref.shape[0]
    T0 = SUBLANES
    cu = c_ref[...].astype(F32) * u_ref[...].astype(F32)
    buf_ref[0:T0, :] = jnp.zeros((T0, cu.shape[1]), F32)
    buf_ref[T0:T0 + L, :] = cu
    conv = cu * w_ref[SC_WIDTH - 1:SC_WIDTH, :]
    for j in range(SC_WIDTH - 1):
        conv = conv + buf_ref[pl.ds(T0 - (SC_WIDTH - 1) + j, L), :] * w_ref[j:j + 1, :]
    a_ref[...] = (b_ref[...].astype(F32) * conv).astype(a_ref.dtype)
    st_ref[0] = buf_ref[pl.ds(T0 + L - (SC_WIDTH - 1), SC_WIDTH - 1), :]


def _sc_prompt(bg, cg, u, conv_w, n_seq):
    t, d = cg.shape
    s = t // n_seq
    wl = 256
    spec = pl.BlockSpec((s, wl), lambda b, j: (b, j))
    return pl.pallas_call(
        _sc_prompt_kernel,
        out_shape=[jax.ShapeDtypeStruct((t, d), BF16),
                   jax.ShapeDtypeStruct((n_seq, SC_WIDTH - 1, d), F32)],
        grid=(n_seq, d // wl),
        in_specs=[spec, spec, spec, pl.BlockSpec((SC_WIDTH, wl), lambda b, j: (0, j))],
        out_specs=[spec, pl.BlockSpec((1, SC_WIDTH - 1, wl), lambda b, j: (b, 0, j))],
        scratch_shapes=[pltpu.VMEM((s + SUBLANES, wl), F32)],
        compiler_params=_params(2 * 3 * s * wl * 4 + 2 * s * wl * 2 + 6 * s * wl * 4, 2),
        name="short_conv_prompt",
    )(bg, cg, u, conv_w)


def _sc_sample_kernel(b_ref, c_ref, u_ref, st_ref, w_ref, a_ref, sto_ref, *, n_new, d):
    up = [st_ref[:, j * d:(j + 1) * d] for j in range(SC_WIDTH - 1)]
    up += [c_ref[:, t * d:(t + 1) * d].astype(F32) * u_ref[:, t * d:(t + 1) * d].astype(F32) for t in range(n_new)]
    for t in range(n_new):
        conv = up[t] * w_ref[0:1, :]
        for j in range(1, SC_WIDTH):
            conv = conv + up[t + j] * w_ref[j:j + 1, :]
        a_ref[:, t * d:(t + 1) * d] = (b_ref[:, t * d:(t + 1) * d].astype(F32) * conv).astype(a_ref.dtype)
    for j in range(SC_WIDTH - 1):
        sto_ref[:, j * d:(j + 1) * d] = up[n_new + j]


def _sc_sample(bg, cg, u, state, conv_w, n_seq):
    t, d = cg.shape
    n_new = t // n_seq
    wide = lambda x: x.reshape(n_seq, n_new * d)
    a, st = pl.pallas_call(
        functools.partial(_sc_sample_kernel, n_new=n_new, d=d),
        out_shape=[jax.ShapeDtypeStruct((n_seq, n_new * d), BF16),
                   jax.ShapeDtypeStruct((n_seq, (SC_WIDTH - 1) * d), F32)],
        compiler_params=pltpu.CompilerParams(vmem_limit_bytes=_vmem_limit(16 * n_seq * n_new * d * 4)),
        name="short_conv_sample",
    )(wide(bg), wide(cg), wide(u), state.reshape(n_seq, (SC_WIDTH - 1) * d), conv_w)
    return a.reshape(t, d), st.reshape(n_seq, SC_WIDTH - 1, d)


def _post_mix_kernel(*refs, n_in):
    a_refs = refs[:n_in]
    w_refs = refs[n_in:2 * n_in]
    x_ref, gate_ref, g_ref, shift_ref, scale_ref, wr_ref, br_ref, xmid_ref, h_ref, logit_ref = refs[2 * n_in:]
    acc = _dot(a_refs[0][...], w_refs[0][...])
    for a_ref, w_ref in zip(a_refs[1:], w_refs[1:]):
        acc = acc + _dot(a_ref[...], w_ref[...])
    xm = x_ref[...] + _row(gate_ref) * acc
    xmid_ref[...] = xm
    h = _rmsnorm(xm, g_ref[...]) * (1.0 + _row(scale_ref)) + _row(shift_ref)
    h_ref[...] = h
    logit_ref[...] = _dot4(h, wr_ref[...]) + br_ref[...]


def _post_mix(a_list, w_list, x, g, mod, w_route, b_route, name):
    t, d = x.shape
    tm = TOKEN_TILE
    n_in = len(a_list)
    row = lambda w: pl.BlockSpec((tm, w), lambda i: (i, 0))
    const = lambda shp: pl.BlockSpec(shp, lambda i: (0, 0))
    w_bytes = sum(w.size * 2 for w in w_list)
    return pl.pallas_call(
        functools.partial(_post_mix_kernel, n_in=n_in),
        out_shape=[jax.ShapeDtypeStruct((t, d), F32), jax.ShapeDtypeStruct((t, d), F32),
                   jax.ShapeDtypeStruct((t, LANES), F32)],
        grid=(t // tm,),
        in_specs=[row(a.shape[1]) for a in a_list] + [const(w.shape) for w in w_list]
        + [row(d), mod.spec(2, tm), const((1, d)), mod.spec(3, tm), mod.spec(4, tm), const((d, LANES)), const((1, LANES))],
        out_specs=[row(d), row(d), row(LANES)],
        compiler_params=_params(2 * w_bytes + 2 * tm * d * (2 * n_in + 12) + 8 * tm * d * 4, 1),
        name=name,
    )(*a_list, *w_list, x, mod.arr, g.reshape(1, d), mod.arr, mod.arr, w_route, b_route)


def _route_kernel(lg_ref, route_ref, cnt_ref, acc_ref):
    i = pl.program_id(0)
    tm = lg_ref.shape[0]

    @pl.when(i == 0)
    def _():
        acc_ref[...] = jnp.zeros_like(acc_ref)

    lg = lg_ref[...]
    lane = lax.broadcasted_iota(I32, lg.shape, 1)
    neg = -jnp.inf
    gl = jnp.where(lane < N_EXPERT_GROUPS, lg, neg)
    gmax = jnp.max(gl, axis=-1, keepdims=True)
    gidx = jnp.min(jnp.where(gl == gmax, lane, LANES), axis=-1, keepdims=True)
    g_top = 1.0 / jnp.sum(jnp.exp(gl - gmax), axis=-1, keepdims=True)
    lo = ROUTER_LANE0 + gidx * EXPERTS_PER_GROUP
    el = jnp.where((lane >= lo) & (lane < lo + EXPERTS_PER_GROUP), lg, neg)
    m1 = jnp.max(el, axis=-1, keepdims=True)
    i1 = jnp.min(jnp.where(el == m1, lane, LANES), axis=-1, keepdims=True)
    el2 = jnp.where(lane == i1, neg, el)
    m2 = jnp.max(el2, axis=-1, keepdims=True)
    i2 = jnp.min(jnp.where(el2 == m2, lane, LANES), axis=-1, keepdims=True)
    p2 = jnp.exp(m2 - m1)
    w1 = g_top / (1.0 + p2)
    w2 = w1 * p2
    sel1 = lane == i1
    sel2 = lane == i2
    onehot = jnp.where(sel1 | sel2, 1.0, 0.0).astype(BF16)
    r = lax.broadcasted_iota(I32, (tm, tm), 0)
    c = lax.broadcasted_iota(I32, (tm, tm), 1)
    tri = jnp.where(c <= r, 1.0, 0.0).astype(BF16)
    cum = _dot(tri, onehot) + acc_ref[...]
    r1 = jnp.sum(jnp.where(sel1, cum, 0.0), axis=-1, keepdims=True) - 1.0
    r2 = jnp.sum(jnp.where(sel2, cum, 0.0), axis=-1, keepdims=True) - 1.0
    acc_ref[...] = cum[tm - 1:tm, :]
    cnt_ref[...] = cum[tm - 1:tm, :]
    e1 = (i1 - ROUTER_LANE0).astype(F32)
    e2 = (i2 - ROUTER_LANE0).astype(F32)
    rec = jnp.zeros(lg.shape, F32)
    for k, val in enumerate((e1, e2, r1, r2, w1, w2)):
        rec = jnp.where(lane == k, val, rec)
    route_ref[...] = rec


def _route(logits):
    t = logits.shape[0]
    tm = TOKEN_TILE
    return pl.pallas_call(
        _route_kernel,
        out_shape=[jax.ShapeDtypeStruct((t, LANES), F32), jax.ShapeDtypeStruct((1, LANES), F32)],
        grid=(t // tm,),
        in_specs=[pl.BlockSpec((tm, LANES), lambda i: (i, 0))],
        out_specs=[pl.BlockSpec((tm, LANES), lambda i: (i, 0)), pl.BlockSpec((1, LANES), lambda i: (0, 0))],
        scratch_shapes=[pltpu.VMEM((1, LANES), F32)],
        compiler_params=_params(64 * tm * LANES * 4, 1),
        name="moe_route",
    )(logits)


def _row_copy(src_ref, src_row, dst_ref, dst_row, sem):
    return pltpu.make_async_copy(src_ref.at[pl.ds(src_row, 1)], dst_ref.at[pl.ds(dst_row, 1)], sem)


def _dispatch_kernel(dest_ref, h_ref, xs_in_ref, xs_ref, sem):
    del xs_in_ref
    tm = h_ref.shape[0]

    def start(r, _):
        for k in range(TOP_K):
            _row_copy(h_ref, r, xs_ref, dest_ref[k * tm + r], sem).start(priority=k % 2)
        return 0

    lax.fori_loop(0, tm, start, 0, unroll=ROW_DMA_UNROLL)
    for k in range(TOP_K):
        pltpu.make_async_copy(h_ref, xs_ref.at[pl.ds(0, tm)], sem).wait()


def _dispatch(h, dest, xs):
    t, d = h.shape
    tm = min(ROW_DMA_TILE, t)
    return pl.pallas_call(
        _dispatch_kernel,
        out_shape=jax.ShapeDtypeStruct(xs.shape, xs.dtype),
        grid=(t // tm,),
        in_specs=[pl.BlockSpec((TOP_K * tm,), lambda i: (i,), memory_space=pltpu.SMEM),
                  pl.BlockSpec((tm, d), lambda i: (i, 0)),
                  pl.BlockSpec(memory_space=pl.ANY)],
        out_specs=pl.BlockSpec(memory_space=pl.ANY),
        scratch_shapes=[pltpu.SemaphoreType.DMA],
        input_output_aliases={2: 0},
        compiler_params=_params(4 * tm * d * 4, 1),
        name="moe_dispatch",
    )(dest, h, xs)


def _experts_kernel(te_ref, nu_ref, xs_ref, wg_ref, wu_ref, wd_ref, ys_ref):
    i = pl.program_id(0)

    @pl.when(i < nu_ref[0])
    def _():
        x = xs_ref[...].astype(BF16)
        hg = _dot(x, wg_ref[...].astype(BF16))
        hu = _dot(x, wu_ref[...].astype(BF16))
        hid = (_silu(hg) * hu).astype(BF16)
        ys_ref[...] = _dot(hid, wd_ref[...].astype(BF16))

    @pl.when(i >= nu_ref[0])
    def _():
        ys_ref[...] = jnp.zeros_like(ys_ref)


def _experts(xs, tile_expert, n_used, w_gate, w_up, w_down, layer):
    p, d = xs.shape
    tm = MOE_TILE
    f = w_gate.shape[-1]
    row_map = lambda i, te, nu: (jnp.minimum(i, jnp.maximum(nu[0] - 1, 0)), 0)
    return pl.pallas_call(
        _experts_kernel,
        out_shape=jax.ShapeDtypeStruct((p, d), F32),
        grid_spec=pltpu.PrefetchScalarGridSpec(
            num_scalar_prefetch=2,
            grid=(p // tm,),
            in_specs=[pl.BlockSpec((tm, d), row_map),
                      pl.BlockSpec((None, None, d, f), lambda i, te, nu: (layer, te[i], 0, 0)),
                      pl.BlockSpec((None, None, d, f), lambda i, te, nu: (layer, te[i], 0, 0)),
                      pl.BlockSpec((None, None, f, d), lambda i, te, nu: (layer, te[i], 0, 0))],
            out_specs=pl.BlockSpec((tm, d), lambda i, te, nu: (i, 0))),
        compiler_params=_params(2 * 3 * d * f * 4 + 4 * tm * d * 4 + 3 * d * f * 2 + 8 * tm * f * 4, 1),
        name="moe_experts",
    )(tile_expert, n_used, xs, w_gate, w_up, w_down)


def _combine_kernel(*refs, final):
    if final:
        dest_ref, route_ref, x_ref, gate_ref, fg_ref, ys_ref, o_ref, y_ref, gbuf_ref, sem = refs
    else:
        dest_ref, route_ref, x_ref, gate_ref, ys_ref, o_ref, gbuf_ref, sem = refs
    tm = x_ref.shape[0]

    def start(r, _):
        for k in range(TOP_K):
            _row_copy(ys_ref, dest_ref[k * tm + r], gbuf_ref.at[k], r, sem).start(priority=k % 2)
        return 0

    lax.fori_loop(0, tm, start, 0, unroll=ROW_DMA_UNROLL)
    for k in range(TOP_K):
        pltpu.make_async_copy(ys_ref.at[pl.ds(0, tm)], gbuf_ref.at[k], sem).wait()
    route = route_ref[...]
    moe = gbuf_ref[0] * route[:, ROUTE_LANE_W:ROUTE_LANE_W + 1]
    for k in range(1, TOP_K):
        moe = moe + gbuf_ref[k] * route[:, ROUTE_LANE_W + k:ROUTE_LANE_W + k + 1]
    x = x_ref[...] + _row(gate_ref) * moe
    o_ref[...] = x
    if final:
        y_ref[...] = _rmsnorm(x, fg_ref[...])


def _combine(dest, route, x_mid, mod, ys, final_g):
    t, d = x_mid.shape
    tm = min(ROW_DMA_TILE, t)
    final = final_g is not None
    row = lambda w: pl.BlockSpec((tm, w), lambda i: (i, 0))
    in_specs = [pl.BlockSpec((TOP_K * tm,), lambda i: (i,), memory_space=pltpu.SMEM),
                row(LANES), row(d), mod.spec(5, tm)]
    args = [dest, route, x_mid, mod.arr]
    if final:
        in_specs.append(pl.BlockSpec((1, d), lambda i: (0, 0)))
        args.append(final_g.reshape(1, d))
    in_specs.append(pl.BlockSpec(memory_space=pl.ANY))
    args.append(ys)
    out_shape = [jax.ShapeDtypeStruct((t, d), F32)] * (2 if final else 1)
    out_specs = [row(d)] * (2 if final else 1)
    return pl.pallas_call(
        functools.partial(_combine_kernel, final=final),
        out_shape=out_shape,
        grid=(t // tm,),
        in_specs=in_specs,
        out_specs=out_specs,
        scratch_shapes=[pltpu.VMEM((TOP_K, tm, d), F32), pltpu.SemaphoreType.DMA],
        compiler_params=_params(8 * tm * d * 4 + TOP_K * tm * d * 4, 1),
        name="moe_combine",
    )(*args)


def _moe(layer, h_p, h_s, logit_p, logit_s, xmid_p, xmid_s, mod_p, mod_s, w_gate, w_up, w_down, final_g):
    tp, d = h_p.shape
    ts = h_s.shape[0]
    route, counts = _route(jnp.concatenate([logit_p, logit_s], axis=0))
    cnt = counts[0, ROUTER_LANE0:ROUTER_LANE0 + N_EXPERTS].astype(I32)
    padded = (cnt + MOE_TILE - 1) // MOE_TILE * MOE_TILE
    ends = jnp.cumsum(padded)
    starts = ends - padded
    rec = route[:, :SUBLANES].T
    expert = rec[ROUTE_LANE_E:ROUTE_LANE_E + TOP_K].astype(I32)
    rank = rec[ROUTE_LANE_R:ROUTE_LANE_R + TOP_K].astype(I32)
    is_e = expert[None] == jnp.arange(N_EXPERTS, dtype=I32)[:, None, None]
    dest = jnp.sum(jnp.where(is_e, starts[:, None, None], 0), axis=0) + rank

    def by_tile(a):
        tm = min(ROW_DMA_TILE, a.shape[1])
        return jnp.transpose(a.reshape(TOP_K, a.shape[1] // tm, tm), (1, 0, 2)).reshape(-1)

    dest_p, dest_s = by_tile(dest[:, :tp]), by_tile(dest[:, tp:])
    n_rows = (tp + ts) * TOP_K + N_EXPERTS * MOE_TILE
    n_tiles = n_rows // MOE_TILE
    n_used = (ends[-1:] // MOE_TILE).astype(I32)
    tile_start = jnp.arange(n_tiles, dtype=I32) * MOE_TILE
    tile_expert = jnp.minimum(jnp.sum((ends[None, :] <= tile_start[:, None]).astype(I32), axis=1), N_EXPERTS - 1)
    xs = jnp.zeros((n_rows, d), F32)
    xs = _dispatch(h_p, dest_p, xs)
    xs = _dispatch(h_s, dest_s, xs)
    ys = _experts(xs, tile_expert, n_used, w_gate, w_up, w_down, layer)
    out_p = _combine(dest_p, route[:tp], xmid_p, mod_p, ys, final_g)
    out_s = _combine(dest_s, route[tp:], xmid_s, mod_s, ys, final_g)
    return out_p, out_s


def kernel(x_prompt, x_sample, c_prompt, c_sample, state_ssm, state_ssd_conv, cache_k, cache_v, page_table, state_short_conv, ada_w, ada_b, ln_mix_g, ln_ffn_g, mix_w_in, mix_w_out, ssd_conv_w, ssd_conv_b, ssd_dt_bias, ssd_a_log, ssd_d, ssd_norm_g, sb_bias, sc_w_in, sc_conv_w, sc_w_out, moe_w_group, moe_b_group, moe_w_router, moe_b_router, moe_w_gate, moe_w_up, moe_w_down, final_g):
    b, s, d = x_prompt.shape
    bs, ls, _ = x_sample.shape
    tp, ts = b * s, bs * ls
    depth = ada_w.shape[0]
    xp = x_prompt.reshape(tp, d)
    xs = x_sample.reshape(ts, d)
    mod = _adaln(jnp.concatenate([c_prompt, c_sample], axis=0), ada_w, ada_b)
    ls_pad = SUBLANES

    def pad_rows(a, rows, front=0):
        return jnp.pad(a, ((0, 0), (front, rows - a.shape[1] - front), (0, 0)))

    ssm_p, ssm_s, cv_p, cv_s, k_p, k_s, v_p, v_s, sc_p, sc_s = ([] for _ in range(10))
    y_out = None
    for layer in range(depth):
        i = layer // 2
        mod_p = _Mod(mod[layer, :b].reshape(b, 1, 6 * d), s, d)
        mod_s = _Mod(jnp.repeat(mod[layer, b:], ls, axis=0), ls, d)
        n_route = N_EXPERT_GROUPS + N_EXPERTS
        w_route = jnp.pad(jnp.concatenate([moe_w_group[layer], moe_w_router[layer]], axis=1),
                          ((0, 0), (0, LANES - n_route)))
        b_route = jnp.pad(jnp.concatenate([moe_b_group[layer], moe_b_router[layer]]),
                          (0, LANES - n_route)).reshape(1, LANES)
        if layer % 2 == 0:
            w = mix_w_in[i]
            o_dt = D_SSM + CONV_DIM
            o_q = o_dt + SSM_HEADS
            w_bf = jnp.concatenate(
                [w[:, :o_dt], w[:, o_q:o_q + D_ATT] * SB_SCALE, w[:, o_q + D_ATT:], w[:, o_dt:o_q],
                 jnp.zeros((d, LANES - SSM_HEADS), F32)], axis=1).astype(BF16)
            c0 = D_SSM + CONV_DIM
            splits = lambda kv, zt: ((0, D_SSM, (zt,)), (D_SSM, CONV_DIM, (F32,)), (c0, D_ATT, (BF16,)),
                                     (c0 + D_ATT, D_ATT, (kv, BF16)), (c0 + 2 * D_ATT, D_ATT, (kv, BF16)),
                                     (c0 + 3 * D_ATT, LANES, (F32,)))
            z1, xbc1, q1, kt1, kb1, vt1, vb1, dt1 = _pre_mix(
                xp, ln_mix_g[layer], mod_p, w_bf, splits(POSITION_LAST, BF16), "pre_mix_even_p")
            z2, xbc2, q2, kf2, kb2, vf2, vb2, dt2 = _pre_mix(
                xs, ln_mix_g[layer], mod_s, w_bf, splits(F32, F32), "pre_mix_even_s")
            prm = (ssd_conv_w[i], ssd_conv_b[i].reshape(1, CONV_DIM),
                   jnp.pad(ssd_dt_bias[i], (0, LANES - SSM_HEADS)).reshape(1, LANES),
                   jnp.pad(ssd_a_log[i], (0, LANES - SSM_HEADS)).reshape(1, LANES),
                   jnp.repeat(ssd_d[i], SSM_HEAD_DIM).reshape(1, D_SSM),
                   ssd_norm_g[i].reshape(1, D_SSM))
            y1, hfin1, cv1 = _ssd(xbc1, z1, dt1, prm, b, SSD_CHUNK, SSD_CHUNK, name="ssd_prompt")
            seq_pad = lambda a: pad_rows(a.reshape(bs, ls, a.shape[-1]), ls_pad).reshape(bs * ls_pad, a.shape[-1])
            y2, hfin2, cv2 = _ssd(
                seq_pad(xbc2), seq_pad(z2), seq_pad(dt2), prm, bs, ls_pad, ls,
                tail0=pad_rows(state_ssd_conv[i], SUBLANES, front=SUBLANES - (SSD_CONV - 1)),
                h0=state_ssm[i].reshape(bs, D_SSM, D_STATE), name="ssd_sample")
            y2 = y2.reshape(bs, ls_pad, D_SSM)[:, :ls].reshape(ts, D_SSM)
            att1 = _attn_prompt(q1, kb1, vb1, sb_bias[i], b)
            pos_last = lambda a: jnp.transpose(a, (0, 1, 3, 4, 2))
            att2 = _attn_sample(
                q2.reshape(bs, ls, D_ATT), pad_rows(kf2.reshape(bs, ls, D_ATT), SUBLANES),
                pad_rows(vf2.reshape(bs, ls, D_ATT), SUBLANES),
                pos_last(cache_k), pos_last(cache_v), i, page_table, sb_bias[i]).reshape(ts, D_ATT)
            w_out = mix_w_out[i].astype(BF16)
            w_list = [w_out[:D_SSM], w_out[D_SSM:]]
            xm1, hf1, lg1 = _post_mix([y1, att1], w_list, xp, ln_ffn_g[layer], mod_p, w_route, b_route, "post_mix_even_p")
            xm2, hf2, lg2 = _post_mix([y2, att2], w_list, xs, ln_ffn_g[layer], mod_s, w_route, b_route, "post_mix_even_s")
            ssm_p.append(hfin1.reshape(b, SSM_HEADS, SSM_HEAD_DIM, D_STATE))
            ssm_s.append(hfin2.reshape(bs, SSM_HEADS, SSM_HEAD_DIM, D_STATE))
            cv_p.append(cv1)
            cv_s.append(cv2)
            seq_first = lambda a: jnp.transpose(a.reshape(b, ATT_HEADS, ATT_HEAD_DIM, s), (0, 3, 1, 2))
            k_p.append(seq_first(kt1))
            k_s.append(kf2.reshape(bs, ls, ATT_HEADS, ATT_HEAD_DIM))
            v_p.append(seq_first(vt1))
            v_s.append(vf2.reshape(bs, ls, ATT_HEADS, ATT_HEAD_DIM))
        else:
            w_bf = sc_w_in[i].astype(BF16)
            splits = tuple((j * d, d, (BF16,)) for j in range(3))
            bg1, cg1, u1 = _pre_mix(xp, ln_mix_g[layer], mod_p, w_bf, splits, "pre_mix_odd_p")
            bg2, cg2, u2 = _pre_mix(xs, ln_mix_g[layer], mod_s, w_bf, splits, "pre_mix_odd_s")
            a1, st1 = _sc_prompt(bg1, cg1, u1, sc_conv_w[i], b)
            a2, st2 = _sc_sample(bg2, cg2, u2, state_short_conv[i], sc_conv_w[i], bs)
            w_list = [sc_w_out[i].astype(BF16)]
            xm1, hf1, lg1 = _post_mix([a1], w_list, xp, ln_ffn_g[layer], mod_p, w_route, b_route, "post_mix_odd_p")
            xm2, hf2, lg2 = _post_mix([a2], w_list, xs, ln_ffn_g[layer], mod_s, w_route, b_route, "post_mix_odd_s")
            sc_p.append(st1)
            sc_s.append(st2)
        fg = final_g if layer == depth - 1 else None
        out_p, out_s = _moe(layer, hf1, hf2, lg1, lg2, xm1, xm2, mod_p, mod_s,
                            moe_w_gate, moe_w_up, moe_w_down, fg)
        xp, xs = out_p[0], out_s[0]
        if fg is not None:
            y_out = (out_p[1].reshape(b, s, d), out_s[1].reshape(bs, ls, d))
    return (y_out[0], y_out[1], jnp.stack(ssm_p), jnp.stack(ssm_s), jnp.stack(cv_p), jnp.stack(cv_s),
            jnp.stack(k_p), jnp.stack(k_s), jnp.stack(v_p), jnp.stack(v_s), jnp.stack(sc_p), jnp.stack(sc_s))
```
